```python
import math
import jax, jax.numpy as jnp
from jax import lax
import numpy as np

D_MODEL = 2048
BATCH = 16
SEQ = 2048
DEPTH = 4

CHUNK = 64
SB_BLOCK = 128
N_MEM = 256
D_MIX = D_MODEL
N_GROUPS = 4
GROUP_WIDTH = D_MIX // N_GROUPS
N_GROUP_HEADS = 4
HEAD_DIM = GROUP_WIDTH // N_GROUP_HEADS
GLA_DK = HEAD_DIM // 2
GLA_LOWRANK = 16
GLA_NORMALIZER = 16.0
DN_CONV = 4
ROPE_THETA = 10000.0
N_CROSS_HEADS = 4
CROSS_HEAD_DIM = D_MODEL // N_CROSS_HEADS
D_FF = 4 * D_MODEL
NORM_EPS = 1e-6

RET_COLS = (GROUP_WIDTH, GROUP_WIDTH, GROUP_WIDTH, GROUP_WIDTH)
GLA_COLS = (N_GROUP_HEADS * GLA_DK, N_GROUP_HEADS * GLA_DK, GROUP_WIDTH, GROUP_WIDTH, GLA_LOWRANK)
SB_COLS = (GROUP_WIDTH, GROUP_WIDTH, GROUP_WIDTH)
DN_COLS = (3 * GROUP_WIDTH, GROUP_WIDTH, N_GROUP_HEADS, N_GROUP_HEADS)
IN_SPLITS = RET_COLS + GLA_COLS + SB_COLS + DN_COLS
D_IN = sum(IN_SPLITS)

kernel_name = "hybrid_parallel_group_streaming_encoder"


def rms_norm(x, gain):
    xf = x.astype(jnp.float32)
    y = xf * lax.rsqrt(jnp.mean(xf * xf, axis=-1, keepdims=True) + NORM_EPS)
    return (y * gain.astype(jnp.float32)).astype(x.dtype)


def head_rms_norm(o, gain):
    b, s, h, d = o.shape
    return rms_norm(o, gain.reshape(h, d)).reshape(b, s, h * d)


def split_heads(t):
    return t.reshape(t.shape[0], t.shape[1], N_GROUP_HEADS, -1)


def l2_normalize(t):
    return t * lax.rsqrt(jnp.sum(t * t, axis=-1, keepdims=True) + 1e-6)


def chunk_heads(t):
    b, s, h, d = t.shape
    return t.reshape(b, s // CHUNK, CHUNK, h, d).transpose(1, 0, 3, 2, 4)


def unchunk_heads(t):
    n, b, h, c, d = t.shape
    return t.transpose(1, 0, 3, 2, 4).reshape(b, n * c, h, d)


def chunk_scalars(t):
    b, s, h = t.shape
    return t.reshape(b, s // CHUNK, CHUNK, h).transpose(1, 0, 3, 2)


def rotary_tables(positions):
    inv_freq = ROPE_THETA ** (-jnp.arange(0, HEAD_DIM, 2, dtype=jnp.float32) / HEAD_DIM)
    ang = positions.astype(jnp.float32)[..., None] * inv_freq
    return jnp.cos(ang)[:, :, None, :], jnp.sin(ang)[:, :, None, :]


def apply_rotary(t, cos, sin):
    t1, t2 = jnp.split(t, 2, axis=-1)
    return jnp.concatenate([t1 * cos - t2 * sin, t1 * sin + t2 * cos], axis=-1)


def retention_mixer(q, k, v, cos, sin):
    f32 = jnp.float32
    q = apply_rotary(q.astype(f32), cos, sin)
    k = apply_rotary(k.astype(f32), cos, sin) * HEAD_DIM ** -0.5
    v = v.astype(f32)
    log_gamma = jnp.log1p(-jnp.exp2(-5.0 - jnp.arange(N_GROUP_HEADS, dtype=f32)))
    pos = jnp.arange(CHUNK, dtype=f32)
    intra_decay = jnp.exp(log_gamma[:, None, None] * jnp.abs(pos[:, None] - pos[None, :]))
    query_decay = jnp.exp(log_gamma[:, None] * (pos + 1.0))[..., None]
    key_decay = jnp.exp(log_gamma[:, None] * (CHUNK - 1.0 - pos))[..., None]
    chunk_decay = jnp.exp(log_gamma * CHUNK)[:, None, None]

    def step(state, inp):
        qc, kc, vc = inp
        scores = jnp.einsum('bhid,bhjd->bhij', qc, kc) * intra_decay
        out = (jnp.einsum('bhij,bhje->bhie', scores, vc)
               + jnp.einsum('bhid,bhde->bhie', qc, state) * query_decay)
        state = chunk_decay * state + jnp.einsum('bhjd,bhje->bhde', kc * key_decay, vc)
        return state, out

    b, _, h, d = q.shape
    state0 = jnp.zeros((b, h, d, v.shape[-1]), f32)
    _, out = lax.scan(step, state0, (chunk_heads(q), chunk_heads(k), chunk_heads(v)))
    return unchunk_heads(out)


def gla_mixer(q, k, v, log_gate):
    f32 = jnp.float32
    q = q.astype(f32) * GLA_DK ** -0.5
    k = k.astype(f32)
    v = v.astype(f32)
    log_gate = log_gate.astype(f32)
    tri = jnp.tril(jnp.ones((CHUNK, CHUNK), dtype=bool))[:, :, None]

    def step(state, inp):
        qc, kc, vc, gc = inp
        b_cum = jnp.cumsum(gc, axis=2)
        diff = b_cum[:, :, :, None, :] - b_cum[:, :, None, :, :]
        pair_decay = jnp.where(tri, jnp.exp(jnp.where(tri, diff, 0.0)), 0.0)
        scores = jnp.einsum('bhijd,bhjd->bhij', qc[:, :, :, None, :] * pair_decay, kc)
        b_last = b_cum[:, :, -1:, :]
        out = (jnp.einsum('bhij,bhje->bhie', scores, vc)
               + jnp.einsum('bhid,bhde->bhie', qc * jnp.exp(b_cum), state))
        state = (jnp.exp(b_last[:, :, 0, :])[..., None] * state
                 + jnp.einsum('bhjd,bhje->bhde', kc * jnp.exp(b_last - b_cum), vc))
        return state, out

    b, _, h, dk = q.shape
    state0 = jnp.zeros((b, h, dk, v.shape[-1]), f32)
    _, out = lax.scan(step, state0, (chunk_heads(q), chunk_heads(k), chunk_heads(v), chunk_heads(log_gate)))
    return unchunk_heads(out)


def stick_breaking_mixer(q, k, v):
    f32 = jnp.float32
    q = q.astype(f32).transpose(0, 2, 1, 3)
    k = k.astype(f32).transpose(0, 2, 1, 3)
    v = v.astype(f32).transpose(0, 2, 1, 3)
    seq = q.shape[2]
    scale = HEAD_DIM ** -0.5
    outs = []
    for blk in range(seq // SB_BLOCK):
        start, end = blk * SB_BLOCK, (blk + 1) * SB_BLOCK
        logits = jnp.einsum('bhtd,bhsd->bhts', q[:, :, start:end], k[:, :, :end]) * scale
        t_idx = start + jnp.arange(SB_BLOCK)
        s_idx = jnp.arange(end)
        mask = s_idx[None, :] < t_idx[:, None]
        log_beta = jax.nn.log_sigmoid(logits)
        log_keep = jnp.where(mask, jax.nn.log_sigmoid(-logits), 0.0)
        log_later = lax.cumsum(log_keep, axis=3, reverse=True) - log_keep
        weights = jnp.where(mask, jnp.exp(log_beta + log_later), 0.0)
        outs.append(jnp.einsum('bhts,bhsd->bhtd', weights, v[:, :, :end]))
    return jnp.concatenate(outs, axis=2).transpose(0, 2, 1, 3)


def causal_depthwise_conv(x, w):
    return lax.conv_general_dilated(
        x, w.astype(x.dtype)[:, None, :], window_strides=(1,), padding=((DN_CONV - 1, 0),),
        dimension_numbers=('NWC', 'WIO', 'NWC'), feature_group_count=x.shape[-1])


def gated_deltanet_mixer(q, k, v, beta, log_decay):
    f32 = jnp.float32
    q = l2_normalize(q.astype(f32)) * HEAD_DIM ** -0.5
    k = l2_normalize(k.astype(f32))
    v = v.astype(f32)
    dv = v.shape[-1]
    tri_incl = jnp.tril(jnp.ones((CHUNK, CHUNK), dtype=bool))
    strict = jnp.tril(jnp.ones((CHUNK, CHUNK), dtype=f32), k=-1)
    eye = jnp.eye(CHUNK, dtype=f32)

    def step(state, inp):
        qc, kc, vc, bc, gc = inp
        g_cum = jnp.cumsum(gc, axis=-1)
        diff = g_cum[..., :, None] - g_cum[..., None, :]
        decay = jnp.where(tri_incl, jnp.exp(jnp.where(tri_incl, diff, 0.0)), 0.0)
        kb = kc * bc[..., None]
        lower = jnp.einsum('bhid,bhjd->bhij', kb, kc) * decay * strict
        rhs = jnp.concatenate([vc * bc[..., None], kb * jnp.exp(g_cum)[..., None]], axis=-1)
        sol = lax.linalg.triangular_solve(eye + lower, rhs, left_side=True, lower=True, unit_diagonal=True)
        u, w = sol[..., :dv], sol[..., dv:]
        v_new = u - jnp.einsum('bhcd,bhde->bhce', w, state)
        scores = jnp.einsum('bhid,bhjd->bhij', qc, kc) * decay
        out = (jnp.einsum('bhid,bhde->bhie', qc * jnp.exp(g_cum)[..., None], state)
               + jnp.einsum('bhij,bhje->bhie', scores, v_new))
        g_last = g_cum[..., -1]
        state = (jnp.exp(g_last)[..., None, None] * state
                 + jnp.einsum('bhjd,bhje->bhde', kc * jnp.exp(g_last[..., None] - g_cum)[..., None], v_new))
        return state, out

    b, _, h, dk = q.shape
    state0 = jnp.zeros((b, h, dk, dv), f32)
    _, out = lax.scan(step, state0, (chunk_heads(q), chunk_heads(k), chunk_heads(v),
                                     chunk_scalars(beta.astype(f32)), chunk_scalars(log_decay.astype(f32))))
    return unchunk_heads(out)


def hybrid_mixer(h, cos, sin, w_in, gla_w_up, gla_b_up, dn_conv_w, dn_a_log, dn_dt_bias, head_gain, w_out):
    dt = h.dtype
    proj = jnp.einsum('bsd,de->bse', h, w_in)
    split_points = np.cumsum(IN_SPLITS)[:-1].tolist()
    (ret_q, ret_k, ret_v, ret_g,
     gla_q, gla_k, gla_v, gla_g, gla_lr,
     sb_q, sb_k, sb_v,
     dn_qkv, dn_z, dn_b, dn_a) = jnp.split(proj, split_points, axis=-1)
    gains = jnp.split(head_gain, N_GROUPS)

    ret = retention_mixer(split_heads(ret_q), split_heads(ret_k), split_heads(ret_v), cos, sin).astype(dt)
    y_ret = jax.nn.silu(ret_g) * head_rms_norm(ret, gains[0])

    gla_logit = jnp.einsum('bsr,re->bse', gla_lr, gla_w_up) + gla_b_up
    gla_log_gate = jax.nn.log_sigmoid(gla_logit.astype(jnp.float32)) / GLA_NORMALIZER
    gla = gla_mixer(split_heads(gla_q), split_heads(gla_k), split_heads(gla_v), split_heads(gla_log_gate)).astype(dt)
    y_gla = jax.nn.silu(gla_g) * head_rms_norm(gla, gains[1])

    sb = stick_breaking_mixer(split_heads(sb_q), split_heads(sb_k), split_heads(sb_v)).astype(dt)
    y_sb = head_rms_norm(sb, gains[2])

    dn_qkv = jax.nn.silu(causal_depthwise_conv(dn_qkv, dn_conv_w))
    dn_q, dn_k, dn_v = jnp.split(dn_qkv, 3, axis=-1)
    dn_beta = jax.nn.sigmoid(dn_b.astype(jnp.float32))
    dn_log_decay = -jnp.exp(dn_a_log.astype(jnp.float32)) * jax.nn.softplus(
        dn_a.astype(jnp.float32) + dn_dt_bias.astype(jnp.float32))
    dn = gated_deltanet_mixer(split_heads(dn_q), split_heads(dn_k), split_heads(dn_v), dn_beta, dn_log_decay).astype(dt)
    y_dn = head_rms_norm(dn, gains[3]) * jax.nn.silu(dn_z)

    y = jnp.concatenate([y_ret, y_gla, y_sb, y_dn], axis=-1)
    return jnp.einsum('bse,ed->bsd', y, w_out)


def cross_attention(h, m, w_q, w_kv, w_o):
    b, s, _ = h.shape
    q = jnp.einsum('bsd,de->bse', h, w_q).reshape(b, s, N_CROSS_HEADS, CROSS_HEAD_DIM)
    k, v = jnp.split(jnp.einsum('bmd,de->bme', m, w_kv), 2, axis=-1)
    k = k.reshape(b, m.shape[1], N_CROSS_HEADS, CROSS_HEAD_DIM)
    v = v.reshape(b, m.shape[1], N_CROSS_HEADS, CROSS_HEAD_DIM)
    scores = jnp.einsum('bshd,bmhd->bhsm', q.astype(jnp.float32), k.astype(jnp.float32)) * CROSS_HEAD_DIM ** -0.5
    p = jax.nn.softmax(scores, axis=-1)
    o = jnp.einsum('bhsm,bmhd->bshd', p, v.astype(jnp.float32)).astype(h.dtype).reshape(b, s, D_MODEL)
    return jnp.einsum('bse,ed->bsd', o, w_o)


def squared_relu_mlp(h, w_up, w_down):
    u = jax.nn.relu(jnp.einsum('bsd,df->bsf', h, w_up))
    return jnp.einsum('bsf,fd->bsd', u * u, w_down)


def setup_inputs(seed: int = 0) -> dict:
    key = jax.random.key(seed)
    ks = jax.random.split(key, 21)
    f32 = jnp.float32
    nrm = lambda k, shape, scale: jax.random.normal(k, shape, f32) * scale
    gain = lambda k, shape: 1.0 + 0.02 * jax.random.normal(k, shape, f32)
    offset = jax.random.randint(ks[2], (BATCH, 1), 0, 4096, dtype=jnp.int32)
    positions = offset + jnp.arange(SEQ, dtype=jnp.int32)[None, :]
    dt0 = jnp.exp(jax.random.uniform(ks[9], (DEPTH, N_GROUP_HEADS), f32,
                                     minval=math.log(1e-3), maxval=math.log(1e-1)))
    return {
        "x": jax.random.normal(ks[0], (BATCH, SEQ, D_MODEL), f32),
        "mem": jax.random.normal(ks[1], (BATCH, N_MEM, D_MODEL), f32),
        "positions": positions,
        "ln_mix": gain(ks[3], (DEPTH, D_MODEL)),
        "w_in": nrm(ks[4], (DEPTH, D_MODEL, D_IN), D_MODEL ** -0.5),
        "gla_w_up": nrm(ks[5], (DEPTH, GLA_LOWRANK, N_GROUP_HEADS * GLA_DK), GLA_LOWRANK ** -0.5),
        "gla_b_up": nrm(ks[6], (DEPTH, N_GROUP_HEADS * GLA_DK), 0.02),
        "dn_conv_w": nrm(ks[7], (DEPTH, DN_CONV, 3 * GROUP_WIDTH), DN_CONV ** -0.5),
        "dn_a_log": jnp.log(jax.random.uniform(ks[8], (DEPTH, N_GROUP_HEADS), f32, minval=1.0, maxval=16.0)),
        "dn_dt_bias": dt0 + jnp.log(-jnp.expm1(-dt0)),
        "mix_head_norm": gain(ks[10], (DEPTH, D_MIX)),
        "w_mix_out": nrm(ks[11], (DEPTH, D_MIX, D_MODEL), D_MIX ** -0.5),
        "ln_cross": gain(ks[12], (DEPTH, D_MODEL)),
        "ln_mem": gain(ks[13], (DEPTH, D_MODEL)),
        "w_cross_q": nrm(ks[14], (DEPTH, D_MODEL, D_MODEL), D_MODEL ** -0.5),
        "w_cross_kv": nrm(ks[15], (DEPTH, D_MODEL, 2 * D_MODEL), D_MODEL ** -0.5),
        "w_cross_o": nrm(ks[16], (DEPTH, D_MODEL, D_MODEL), D_MODEL ** -0.5),
        "ln_mlp": gain(ks[17], (DEPTH, D_MODEL)),
        "w_mlp_up": nrm(ks[18], (DEPTH, D_MODEL, D_FF), D_MODEL ** -0.5),
        "w_mlp_down": nrm(ks[19], (DEPTH, D_FF, D_MODEL), D_FF ** -0.5),
        "ln_final": gain(ks[20], (D_MODEL,)),
    }


def reference(x, mem, positions, ln_mix, w_in, gla_w_up, gla_b_up, dn_conv_w, dn_a_log, dn_dt_bias,
              mix_head_norm, w_mix_out, ln_cross, ln_mem, w_cross_q, w_cross_kv, w_cross_o,
              ln_mlp, w_mlp_up, w_mlp_down, ln_final):
    cos, sin = rotary_tables(positions)
    for layer in range(DEPTH):
        h = rms_norm(x, ln_mix[layer])
        x = x + hybrid_mixer(h, cos, sin, w_in[layer], gla_w_up[layer], gla_b_up[layer], dn_conv_w[layer],
                             dn_a_log[layer], dn_dt_bias[layer], mix_head_norm[layer], w_mix_out[layer])
        h = rms_norm(x, ln_cross[layer])
        m = rms_norm(mem, ln_mem[layer])
        x = x + cross_attention(h, m, w_cross_q[layer], w_cross_kv[layer], w_cross_o[layer])
        h = rms_norm(x, ln_mlp[layer])
        x = x + squared_relu_mlp(h, w_mlp_up[layer], w_mlp_down[layer])
    return rms_norm(x, ln_final)
```

```python
import functools
import math

import jax
import jax.numpy as jnp
from jax import lax
from jax.experimental import pallas as pl
from jax.experimental.pallas import tpu as pltpu

F32 = jnp.float32
BF16 = jnp.bfloat16

D_MODEL = 2048
DEPTH = 4
CHUNK = 64
N_MEM = 256
GROUP_WIDTH = 512
N_HEADS = 4
HEAD_DIM = 128
GLA_DK = 64
GLA_LOWRANK = 16
GLA_NORMALIZER = 16.0
DN_CONV = 4
ROPE_THETA = 10000.0
N_CROSS_HEADS = 4
CROSS_HEAD_DIM = 512
D_FF = 4 * D_MODEL
NORM_EPS = 1e-6

N_MAIN = 14 * GROUP_WIDTH
N_SMALL = 128
LANE_BETA = GLA_LOWRANK
LANE_DECAY = GLA_LOWRANK + N_HEADS

SEQ_TILE = 256
MIB = 1024 * 1024


def _cparams(sem, vmem_mib):
    return pltpu.CompilerParams(dimension_semantics=sem, vmem_limit_bytes=vmem_mib * MIB)


def _dot(a, b):
    return jnp.dot(a, b, preferred_element_type=F32)


def _dot_nt(a, b):
    return lax.dot_general(a, b, (((1,), (1,)), ((), ())), preferred_element_type=F32)


def _dot_tn(a, b):
    return lax.dot_general(a, b, (((0,), (0,)), ((), ())), preferred_element_type=F32)


def _split2(x):
    hi = x.astype(BF16)
    lo = (x - hi.astype(F32)).astype(BF16)
    return hi, lo


def _split3(x):
    hi = x.astype(BF16)
    r = x - hi.astype(F32)
    mid = r.astype(BF16)
    lo = (r - mid.astype(F32)).astype(BF16)
    return hi, mid, lo


def _dot_exact_lhs(m_bf16, x):
    hi, mid, lo = _split3(x)
    return _dot(m_bf16, hi) + _dot(m_bf16, mid) + _dot(m_bf16, lo)


def _dot3(a, b):
    ah, al = _split2(a)
    bh, bl = _split2(b)
    return _dot(ah, bh) + (_dot(ah, bl) + _dot(al, bh))


def _sigmoid(x):
    return 1.0 / (1.0 + jnp.exp(-x))


def _silu(x):
    return x * _sigmoid(x)


def _softplus(x):
    return jnp.maximum(x, 0.0) + jnp.log1p(jnp.exp(-jnp.abs(x)))


def _log_sigmoid(x):
    return -_softplus(-x)


def _head_norm_gate(o, gain, gate):
    ms = jnp.mean(o * o, axis=-1, keepdims=True)
    return o * lax.rsqrt(ms + NORM_EPS) * gain * gate


def _matmul(a_parts, w, *, gain=None, res=None, act=None, w_small=None, out_dtype, tm, tn, vmem_mib):
    m = a_parts[0].shape[0]
    k = sum(p.shape[1] for p in a_parts)
    n = w.shape[1]
    tm = min(tm, m)
    tn = min(tn, n)
    assert m % tm == 0 and n % tn == 0 and w.shape[0] == k
    norm = gain is not None
    n_parts = len(a_parts)
    assert not (norm and n_parts != 1)
    rows = min(256, tm)

    def kern(*refs):
        it = iter(refs)
        a_refs = [next(it) for _ in range(n_parts)]
        g_ref = next(it) if norm else None
        w_ref = next(it)
        ws_ref = next(it) if w_small is not None else None
        r_ref = next(it) if res is not None else None
        o_ref = next(it)
        os_ref = next(it) if w_small is not None else None
        h_ref = next(it) if norm else None

        if norm:
            @pl.when(pl.program_id(1) == 0)
            def _():
                def body(r, carry):
                    sl = pl.ds(pl.multiple_of(r * rows, rows), rows)
                    xf = a_refs[0][sl, :]
                    ms = jnp.mean(xf * xf, axis=-1, keepdims=True)
                    h_ref[sl, :] = (xf * lax.rsqrt(ms + NORM_EPS) * g_ref[...]).astype(BF16)
                    return carry
                lax.fori_loop(0, tm // rows, body, 0)
                if w_small is not None:
                    os_ref[...] = _dot(h_ref[...], ws_ref[...])
            lhs = h_ref[...]
        elif n_parts == 1:
            lhs = a_refs[0][...]
        else:
            lhs = jnp.concatenate([r[...] for r in a_refs], axis=-1)
        acc = _dot(lhs, w_ref[...])
        if act == "relu2":
            acc = jnp.maximum(acc, 0.0)
            acc = acc * acc
        if res is not None:
            acc = acc + r_ref[...]
        o_ref[...] = acc.astype(out_dtype)

    in_specs = [pl.BlockSpec((tm, p.shape[1]), lambda i, j: (i, 0)) for p in a_parts]
    args = list(a_parts)
    if norm:
        in_specs.append(pl.BlockSpec((1, k), lambda i, j: (0, 0)))
        args.append(gain.reshape(1, k))
    in_specs.append(pl.BlockSpec((k, tn), lambda i, j: (0, j)))
    args.append(w)
    if w_small is not None:
        in_specs.append(pl.BlockSpec((k, N_SMALL), lambda i, j: (0, 0)))
        args.append(w_small)
    if res is not None:
        in_specs.append(pl.BlockSpec((tm, tn), lambda i, j: (i, j)))
        args.append(res)
    out_shape = [jax.ShapeDtypeStruct((m, n), out_dtype)]
    out_specs = [pl.BlockSpec((tm, tn), lambda i, j: (i, j))]
    if w_small is not None:
        out_shape.append(jax.ShapeDtypeStruct((m, N_SMALL), F32))
        out_specs.append(pl.BlockSpec((tm, N_SMALL), lambda i, j: (i, 0)))
    scratch = [pltpu.VMEM((tm, k), BF16)] if norm else []
    outs = pl.pallas_call(
        kern, grid=(m // tm, n // tn), in_specs=in_specs, out_specs=out_specs, out_shape=out_shape,
        scratch_shapes=scratch, compiler_params=_cparams(("parallel", "arbitrary"), vmem_mib),
    )(*args)
    return outs if w_small is not None else outs[0]


def _final_norm(x, gain, *, tm):
    m, d = x.shape
    tm = min(tm, m)

    def kern(x_ref, g_ref, o_ref):
        xf = x_ref[...]
        ms = jnp.mean(xf * xf, axis=-1, keepdims=True)
        o_ref[...] = xf * lax.rsqrt(ms + NORM_EPS) * g_ref[...]

    return pl.pallas_call(
        kern, grid=(m // tm,),
        in_specs=[pl.BlockSpec((tm, d), lambda i: (i, 0)), pl.BlockSpec((1, d), lambda i: (0, 0))],
        out_specs=pl.BlockSpec((tm, d), lambda i: (i, 0)),
        out_shape=jax.ShapeDtypeStruct((m, d), F32),
        compiler_params=_cparams(("parallel",), 40),
    )(x, gain.reshape(1, d))


def _rotary_tables(positions):
    b, s = positions.shape
    inv_freq = ROPE_THETA ** (-jnp.arange(0, HEAD_DIM, 2, dtype=F32) / HEAD_DIM)
    inv2 = jnp.concatenate([inv_freq, inv_freq]).reshape(1, HEAD_DIM)
    sign = jnp.concatenate([-jnp.ones((HEAD_DIM // 2,), F32), jnp.ones((HEAD_DIM // 2,), F32)]).reshape(1, HEAD_DIM)

    def kern(p_ref, f_ref, s_ref, cos_ref, sin_ref):
        ang = p_ref[0].astype(F32) * f_ref[...]
        cos_ref[0] = jnp.cos(ang)
        sin_ref[0] = jnp.sin(ang) * s_ref[...]

    tbl = jax.ShapeDtypeStruct((b, s, HEAD_DIM), F32)
    return pl.pallas_call(
        kern, grid=(b,),
        in_specs=[pl.BlockSpec((1, s, 1), lambda i: (i, 0, 0)),
                  pl.BlockSpec((1, HEAD_DIM), lambda i: (0, 0)),
                  pl.BlockSpec((1, HEAD_DIM), lambda i: (0, 0))],
        out_specs=[pl.BlockSpec((1, s, HEAD_DIM), lambda i: (i, 0, 0))] * 2,
        out_shape=[tbl, tbl],
        compiler_params=_cparams(("parallel",), 32),
    )(positions.reshape(b, s, 1), inv2, sign)


def _proj_spec(col_block):
    return pl.BlockSpec((1, SEQ_TILE, GROUP_WIDTH), lambda b, s: (b, s, col_block))


def _const_spec(shape):
    nd = len(shape)
    return pl.BlockSpec(shape, lambda b, s: (0,) * nd)


def _chunk_masks():
    t = jnp.arange(SEQ_TILE)
    same = (t[:, None] // CHUNK) == (t[None, :] // CHUNK)
    incl = (same & (t[None, :] <= t[:, None])).astype(F32)
    strict = (same & (t[None, :] < t[:, None])).astype(F32)
    return incl, strict


def _retention(proj, cosf, sinf, gain):
    b, s, _ = proj.shape
    t = SEQ_TILE
    f32 = F32
    log_gamma = jnp.log1p(-jnp.exp2(-5.0 - jnp.arange(N_HEADS, dtype=f32)))
    pos = jnp.arange(t, dtype=f32)
    cpos = jnp.arange(t) % CHUNK
    scale = HEAD_DIM ** -0.5
    dist = jnp.abs(pos[:, None] - pos[None, :])
    allowed = (pos[None, :] <= pos[:, None]) | ((jnp.arange(t)[:, None] // CHUNK) == (jnp.arange(t)[None, :] // CHUNK))
    mask = jnp.where(allowed[None], jnp.exp(log_gamma[:, None, None] * dist[None]), 0.0) * scale
    qdec = jnp.exp(log_gamma[:, None] * (pos + 1.0))[..., None] * scale
    kdec = jnp.exp(log_gamma[:, None] * (t - 1.0 - pos))[..., None]
    cdec = jnp.exp(log_gamma * t)[:, None, None]
    qdec = jnp.broadcast_to(qdec, (N_HEADS, t, HEAD_DIM))
    kdec = jnp.broadcast_to(kdec, (N_HEADS, t, HEAD_DIM))
    cdec = jnp.broadcast_to(cdec, (N_HEADS, 1, HEAD_DIM))
    del cpos

    def kern(q_ref, k_ref, v_ref, g_ref, cos_ref, sin_ref, m_ref, qd_ref, kd_ref, cd_ref, gain_ref, o_ref, st_ref):
        @pl.when(pl.program_id(1) == 0)
        def _():
            st_ref[...] = jnp.zeros_like(st_ref)

        cosv = cos_ref[0]
        sinv = sin_ref[0]
        for h in range(N_HEADS):
            sl = slice(h * HEAD_DIM, (h + 1) * HEAD_DIM)
            q = q_ref[0, :, sl].astype(F32)
            k = k_ref[0, :, sl].astype(F32)
            v = v_ref[0, :, sl]
            qr = q * cosv + pltpu.roll(q, HEAD_DIM // 2, 1) * sinv
            kr = k * cosv + pltpu.roll(k, HEAD_DIM // 2, 1) * sinv
            sc = _dot_nt(qr.astype(BF16), kr.astype(BF16)) * m_ref[h]
            st = st_ref[h]
            o = _dot(sc.astype(BF16), v) + _dot((qr * qd_ref[h]).astype(BF16), st.astype(BF16))
            st_ref[h] = cd_ref[h] * st + _dot_tn((kr * kd_ref[h]).astype(BF16), v)
            gate = _silu(g_ref[0, :, sl].astype(F32))
            o_ref[0, :, sl] = _head_norm_gate(o, gain_ref[:, sl], gate).astype(BF16)

    return pl.pallas_call(
        kern, grid=(b, s // t),
        in_specs=[_proj_spec(0), _proj_spec(1), _proj_spec(2), _proj_spec(3),
                  pl.BlockSpec((1, t, HEAD_DIM), lambda i, j: (i, j, 0)),
                  pl.BlockSpec((1, t, HEAD_DIM), lambda i, j: (i, j, 0)),
                  _const_spec((N_HEADS, t, t)), _const_spec((N_HEADS, t, HEAD_DIM)),
                  _const_spec((N_HEADS, t, HEAD_DIM)), _const_spec((N_HEADS, 1, HEAD_DIM)),
                  _const_spec((1, GROUP_WIDTH))],
        out_specs=pl.BlockSpec((1, t, GROUP_WIDTH), lambda i, j: (i, j, 0)),
        out_shape=jax.ShapeDtypeStruct((b, s, GROUP_WIDTH), BF16),
        scratch_shapes=[pltpu.VMEM((N_HEADS, HEAD_DIM, HEAD_DIM), F32)],
        compiler_params=_cparams(("parallel", "arbitrary"), 32),
    )(proj, proj, proj, proj, cosf, sinf, mask, qdec, kdec, cdec, gain.reshape(1, GROUP_WIDTH))


def _gla(proj, small, w_up, b_up, gain):
    b, s, _ = proj.shape
    t = SEQ_TILE
    n_chunks = t // CHUNK
    incl, _ = _chunk_masks()
    w_up_pad = jnp.zeros((N_SMALL, N_HEADS * GLA_DK), F32).at[:GLA_LOWRANK].set(w_up)
    scale = GLA_DK ** -0.5

    def kern(qk_ref, v_ref, g_ref, sm_ref, wup_ref, bup_ref, tri_ref, gain_ref, o_ref, st_ref):
        @pl.when(pl.program_id(1) == 0)
        def _():
            st_ref[...] = jnp.zeros_like(st_ref)

        logit = _dot3(sm_ref[0], wup_ref[...]) + bup_ref[...]
        lg = _log_sigmoid(logit) * (1.0 / GLA_NORMALIZER)
        tri = tri_ref[...]
        bcum = _dot_exact_lhs(tri.astype(BF16), lg)
        e_pos = jnp.exp(bcum)
        e_neg = jnp.exp(-bcum)
        for h in range(N_HEADS):
            dsl = slice(h * GLA_DK, (h + 1) * GLA_DK)
            vsl = slice(h * HEAD_DIM, (h + 1) * HEAD_DIM)
            q = qk_ref[0, :, dsl].astype(F32) * scale
            k = qk_ref[0, :, N_HEADS * GLA_DK + h * GLA_DK: N_HEADS * GLA_DK + (h + 1) * GLA_DK].astype(F32)
            v = v_ref[0, :, vsl]
            bc = bcum[:, dsl]
            qt = (q * e_pos[:, dsl]).astype(BF16)
            kt = (k * e_neg[:, dsl]).astype(BF16)
            a = _dot_nt(qt, kt) * tri
            o_intra = _dot(a.astype(BF16), v)
            st = st_ref[h]
            outs = []
            for c in range(n_chunks):
                rows = slice(c * CHUNK, (c + 1) * CHUNK)
                outs.append(_dot_nt(qt[rows], st.astype(BF16)))
                b_last = bc[(c + 1) * CHUNK - 1:(c + 1) * CHUNK, :]
                kd = (k[rows] * jnp.exp(b_last - bc[rows])).astype(BF16)
                st = st * jnp.exp(b_last) + _dot_tn(v[rows], kd)
            st_ref[h] = st
            o = o_intra + jnp.concatenate(outs, axis=0)
            gate = _silu(g_ref[0, :, vsl].astype(F32))
            o_ref[0, :, vsl] = _head_norm_gate(o, gain_ref[:, vsl], gate).astype(BF16)

    return pl.pallas_call(
        kern, grid=(b, s // t),
        in_specs=[_proj_spec(4), _proj_spec(5), _proj_spec(6),
                  pl.BlockSpec((1, t, N_SMALL), lambda i, j: (i, j, 0)),
                  _const_spec((N_SMALL, N_HEADS * GLA_DK)), _const_spec((1, N_HEADS * GLA_DK)),
                  _const_spec((t, t)), _const_spec((1, GROUP_WIDTH))],
        out_specs=pl.BlockSpec((1, t, GROUP_WIDTH), lambda i, j: (i, j, 0)),
        out_shape=jax.ShapeDtypeStruct((b, s, GROUP_WIDTH), BF16),
        scratch_shapes=[pltpu.VMEM((N_HEADS, HEAD_DIM, GLA_DK), F32)],
        compiler_params=_cparams(("parallel", "arbitrary"), 32),
    )(proj, proj, proj, small, w_up_pad, b_up.reshape(1, -1), incl, gain.reshape(1, GROUP_WIDTH))


def _stick_breaking(proj, gain):
    b, s, _ = proj.shape
    t = SEQ_TILE
    scale = HEAD_DIM ** -0.5
    idx = jnp.arange(t)
    later = (idx[:, None] > idx[None, :]).astype(BF16)
    causal = (idx[None, :] < idx[:, None]).astype(F32)

    def kern(q_ref, k_ref, v_ref, later_ref, causal_ref, gain_ref, o_ref):
        qi = pl.program_id(1)
        later_m = later_ref[...]

        def block(qh, kb, vb, carry, acc, diag_mask):
            z = _dot_nt(qh, kb) * scale
            sp = jnp.log1p(jnp.exp(-jnp.abs(z)))
            log_beta = jnp.minimum(z, 0.0) - sp
            log_keep = log_beta - z
            if diag_mask is not None:
                log_keep = log_keep * diag_mask
            hi, lo = _split2(log_keep)
            log_later = _dot(hi, later_m) + _dot(lo, later_m) + carry
            w = jnp.exp(log_beta + log_later)
            if diag_mask is not None:
                w = w * diag_mask
            acc = acc + _dot(w.astype(BF16), vb)
            carry = carry + jnp.sum(log_keep, axis=-1, keepdims=True)
            return carry, acc

        for h in range(N_HEADS):
            sl = slice(h * HEAD_DIM, (h + 1) * HEAD_DIM)
            qh = q_ref[0, :, sl]
            row0 = pl.multiple_of(qi * t, t)
            carry0 = jnp.zeros((t, 1), F32)
            acc0 = jnp.zeros((t, HEAD_DIM), F32)
            carry, acc = block(qh, k_ref[0, pl.ds(row0, t), sl], v_ref[0, pl.ds(row0, t), sl],
                               carry0, acc0, causal_ref[...])

            def body(i, ca):
                kb_idx = qi - 1 - i
                r = pl.multiple_of(kb_idx * t, t)
                return block(qh, k_ref[0, pl.ds(r, t), sl], v_ref[0, pl.ds(r, t), sl], ca[0], ca[1], None)

            carry, acc = lax.fori_loop(0, qi, body, (carry, acc))
            o_ref[0, :, sl] = _head_norm_gate(acc, gain_ref[:, sl], 1.0).astype(BF16)

    return pl.pallas_call(
        kern, grid=(b, s // t),
        in_specs=[_proj_spec(7),
                  pl.BlockSpec((1, s, GROUP_WIDTH), lambda i, j: (i, 0, 8)),
                  pl.BlockSpec((1, s, GROUP_WIDTH), lambda i, j: (i, 0, 9)),
                  _const_spec((t, t)), _const_spec((t, t)), _const_spec((1, GROUP_WIDTH))],
        out_specs=pl.BlockSpec((1, t, GROUP_WIDTH), lambda i, j: (i, j, 0)),
        out_shape=jax.ShapeDtypeStruct((b, s, GROUP_WIDTH), BF16),
        compiler_params=_cparams(("parallel", "arbitrary"), 40),
    )(proj, proj, proj, later, causal, gain.reshape(1, GROUP_WIDTH))


def _deltanet(proj, small, conv_w, a_log, dt_bias, gain):
    b, s, _ = proj.shape
    t = SEQ_TILE
    n_chunks = t // CHUNK
    incl, strict = _chunk_masks()
    a_pad = jnp.zeros((1, N_SMALL), F32).at[0, LANE_DECAY:LANE_DECAY + N_HEADS].set(a_log)
    dt_pad = jnp.zeros((1, N_SMALL), F32).at[0, LANE_DECAY:LANE_DECAY + N_HEADS].set(dt_bias)
    sel = jnp.zeros((8, N_SMALL), F32).at[jnp.arange(N_HEADS), LANE_DECAY + jnp.arange(N_HEADS)].set(1.0).astype(BF16)
    qscale = HEAD_DIM ** -0.5
    pad = 8

    def kern(q_ref, k_ref, v_ref, z_ref, sm_ref, cw_ref, alog_ref, dtb_ref, incl_ref, strict_ref, sel_ref, gain_ref,
             o_ref, st_ref, tail_ref):
        @pl.when(pl.program_id(1) == 0)
        def _():
            st_ref[...] = jnp.zeros_like(st_ref)
            tail_ref[...] = jnp.zeros_like(tail_ref)

        convd = []
        for i, ref in enumerate((q_ref, k_ref, v_ref)):
            x = ref[0].astype(F32)
            full = jnp.concatenate([tail_ref[i], x], axis=0)
            csl = slice(i * GROUP_WIDTH, (i + 1) * GROUP_WIDTH)
            y = full[pad:pad + t] * cw_ref[DN_CONV - 1:DN_CONV, csl]
            for tap in range(DN_CONV - 1):
                off = pad - (DN_CONV - 1) + tap
                y = y + full[off:off + t] * cw_ref[tap:tap + 1, csl]
            tail_ref[i] = x[t - pad:]
            convd.append(_silu(y))
        cq, ck, cv = convd

        sm = sm_ref[0]
        beta_all = _sigmoid(sm)
        g_all = -jnp.exp(alog_ref[...]) * _softplus(sm + dtb_ref[...])
        incl_m = incl_ref[...]
        strict_m = strict_ref[...]
        gcum = _dot_exact_lhs(incl_m.astype(BF16), g_all)
        ghi, gmid, glo = _split3(gcum)
        sel_m = sel_ref[...]
        grow = _dot_nt(sel_m, ghi) + _dot_nt(sel_m, gmid) + _dot_nt(sel_m, glo)
        egc = jnp.exp(gcum)

        for h in range(N_HEADS):
            sl = slice(h * HEAD_DIM, (h + 1) * HEAD_DIM)
            gl = LANE_DECAY + h
            q = cq[:, sl]
            k = ck[:, sl]
            v = cv[:, sl]
            q = q * lax.rsqrt(jnp.sum(q * q, axis=-1, keepdims=True) + 1e-6) * qscale
            k = k * lax.rsqrt(jnp.sum(k * k, axis=-1, keepdims=True) + 1e-6)
            beta = beta_all[:, LANE_BETA + h:LANE_BETA + h + 1]
            gc = gcum[:, gl:gl + 1]
            eg = egc[:, gl:gl + 1]
            diff = gc - grow[h:h + 1, :]
            decay = jnp.exp(jnp.minimum(diff, 0.0)) * incl_m
            kb = k * beta
            k16 = k.astype(BF16)
            lower = _dot_nt(kb.astype(BF16), k16) * decay * strict_m
            a = -lower
            p = a
            for _ in range(int(math.log2(CHUNK)) - 1):
                a = _dot3(a, a)
                p = p + a + _dot3(a, p)
            rhs = jnp.concatenate([v * beta, kb * eg], axis=-1)
            sol = rhs + _dot3(p, rhs)
            u = sol[:, :HEAD_DIM]
            w = sol[:, HEAD_DIM:]
            scores = _dot_nt(q.astype(BF16), k16) * decay
            qe = (q * eg).astype(BF16)
            st = st_ref[h]
            v_new = []
            inter = []
            for c in range(n_chunks):
                rows = slice(c * CHUNK, (c + 1) * CHUNK)
                st16 = st.astype(BF16)
                vn = u[rows] - _dot(w[rows].astype(BF16), st16)
                inter.append(_dot(qe[rows], st16))
                g_last = gc[(c + 1) * CHUNK - 1:(c + 1) * CHUNK, :]
                kd = (k[rows] * jnp.exp(g_last - gc[rows])).astype(BF16)
                st = jnp.exp(g_last) * st + _dot_tn(kd, vn.astype(BF16))
                v_new.append(vn)
            st_ref[h] = st
            v_new = jnp.concatenate(v_new, axis=0)
            o = jnp.concatenate(inter, axis=0) + _dot(scores.astype(BF16), v_new.astype(BF16))
            gate = _silu(z_ref[0, :, sl].astype(F32))
            o_ref[0, :, sl] = _head_norm_gate(o, gain_ref[:, sl], gate).astype(BF16)

    return pl.pallas_call(
        kern, grid=(b, s // t),
        in_specs=[_proj_spec(10), _proj_spec(11), _proj_spec(12), _proj_spec(13),
                  pl.BlockSpec((1, t, N_SMALL), lambda i, j: (i, j, 0)),
                  _const_spec((DN_CONV, 3 * GROUP_WIDTH)), _const_spec((1, N_SMALL)), _const_spec((1, N_SMALL)),
                  _const_spec((t, t)), _const_spec((t, t)), _const_spec((8, N_SMALL)), _const_spec((1, GROUP_WIDTH))],
        out_specs=pl.BlockSpec((1, t, GROUP_WIDTH), lambda i, j: (i, j, 0)),
        out_shape=jax.ShapeDtypeStruct((b, s, GROUP_WIDTH), BF16),
        scratch_shapes=[pltpu.VMEM((N_HEADS, HEAD_DIM, HEAD_DIM), F32), pltpu.VMEM((3, pad, GROUP_WIDTH), F32)],
        compiler_params=_cparams(("parallel", "arbitrary"), 40),
    )(proj, proj, proj, proj, small, conv_w, a_pad, dt_pad, incl, strict, sel, gain.reshape(1, GROUP_WIDTH))


def _cross_attention(q, kv, *, tq):
    b, s, d = q.shape
    tq = min(tq, s)
    scale = CROSS_HEAD_DIM ** -0.5

    def kern(q_ref, kv_ref, o_ref):
        for h in range(N_CROSS_HEADS):
            sl = slice(h * CROSS_HEAD_DIM, (h + 1) * CROSS_HEAD_DIM)
            vsl = slice(d + h * CROSS_HEAD_DIM, d + (h + 1) * CROSS_HEAD_DIM)
            sc = _dot_nt(q_ref[0, :, sl], kv_ref[0, :, sl]) * scale
            mx = jnp.max(sc, axis=-1, keepdims=True)
            p = jnp.exp(sc - mx)
            denom = jnp.sum(p, axis=-1, keepdims=True)
            o = _dot(p.astype(BF16), kv_ref[0, :, vsl]) / denom
            o_ref[0, :, sl] = o.astype(BF16)

    return pl.pallas_call(
        kern, grid=(b, s // tq),
        in_specs=[pl.BlockSpec((1, tq, d), lambda i, j: (i, j, 0)),
                  pl.BlockSpec((1, N_MEM, 2 * d), lambda i, j: (i, 0, 0))],
        out_specs=pl.BlockSpec((1, tq, d), lambda i, j: (i, j, 0)),
        out_shape=jax.ShapeDtypeStruct((b, s, d), BF16),
        compiler_params=_cparams(("parallel", "arbitrary"), 40),
    )(q, kv)


def _reorder_w_in(w_in):
    lr0 = 4 * GROUP_WIDTH + 2 * N_HEADS * GLA_DK + 2 * GROUP_WIDTH
    sb0 = lr0 + GLA_LOWRANK
    db0 = sb0 + 3 * GROUP_WIDTH + 3 * GROUP_WIDTH + GROUP_WIDTH
    main = jnp.concatenate([w_in[:, :lr0], w_in[:, sb0:db0]], axis=1).astype(BF16)
    small = jnp.concatenate(
        [w_in[:, lr0:sb0], w_in[:, db0:db0 + 2 * N_HEADS],
         jnp.zeros((w_in.shape[0], N_SMALL - GLA_LOWRANK - 2 * N_HEADS), w_in.dtype)], axis=1).astype(BF16)
    return main, small


def _layer(x, mem, cosf, sinf, p):
    b, s, d = x.shape
    m = b * s
    x2 = x.reshape(m, d)
    w_main, w_small = _reorder_w_in(p["w_in"])
    proj, small = _matmul([x2], w_main, gain=p["ln_mix"], w_small=w_small, out_dtype=BF16,
                          tm=1024, tn=1024, vmem_mib=48)
    proj = proj.reshape(b, s, N_MAIN)
    small = small.reshape(b, s, N_SMALL)
    gains = jnp.split(p["mix_head_norm"], 4)
    y_ret = _retention(proj, cosf, sinf, gains[0])
    y_gla = _gla(proj, small, p["gla_w_up"], p["gla_b_up"], gains[1])
    y_sb = _stick_breaking(proj, gains[2])
    y_dn = _deltanet(proj, small, p["dn_conv_w"], p["dn_a_log"], p["dn_dt_bias"], gains[3])
    parts = [y.reshape(m, GROUP_WIDTH) for y in (y_ret, y_gla, y_sb, y_dn)]
    x2 = _matmul(parts, p["w_mix_out"].astype(BF16), res=x2, out_dtype=F32, tm=1024, tn=1024, vmem_mib=48)

    q = _matmul([x2], p["w_cross_q"].astype(BF16), gain=p["ln_cross"], out_dtype=BF16, tm=1024, tn=1024, vmem_mib=48)
    kv = _matmul([mem.reshape(-1, d)], p["w_cross_kv"].astype(BF16), gain=p["ln_mem"], out_dtype=BF16,
                 tm=1024, tn=1024, vmem_mib=48)
    o = _cross_attention(q.reshape(b, s, d), kv.reshape(b, N_MEM, 2 * d), tq=1024)
    x2 = _matmul([o.reshape(m, d)], p["w_cross_o"].astype(BF16), res=x2, out_dtype=F32, tm=1024, tn=1024, vmem_mib=48)

    u = _matmul([x2], p["w_mlp_up"].astype(BF16), gain=p["ln_mlp"], act="relu2", out_dtype=BF16,
                tm=1024, tn=1024, vmem_mib=48)
    x2 = _matmul([u], p["w_mlp_down"].astype(BF16), res=x2, out_dtype=F32, tm=512, tn=512, vmem_mib=48)
    return x2.reshape(b, s, d)


def kernel(x, mem, positions, ln_mix, w_in, gla_w_up, gla_b_up, dn_conv_w, dn_a_log, dn_dt_bias, mix_head_norm,
           w_mix_out, ln_cross, ln_mem, w_cross_q, w_cross_kv, w_cross_o, ln_mlp, w_mlp_up, w_mlp_down, ln_final):
    cosf, sinf = _rotary_tables(positions)
    stacked = dict(ln_mix=ln_mix, w_in=w_in, gla_w_up=gla_w_up, gla_b_up=gla_b_up, dn_conv_w=dn_conv_w,
                   dn_a_log=dn_a_log, dn_dt_bias=dn_dt_bias, mix_head_norm=mix_head_norm, w_mix_out=w_mix_out,
                   ln_cross=ln_cross, ln_mem=ln_mem, w_cross_q=w_cross_q, w_cross_kv=w_cross_kv,
                   w_cross_o=w_cross_o, ln_mlp=ln_mlp, w_mlp_up=w_mlp_up, w_mlp_down=w_mlp_down)
    for layer in range(ln_mix.shape[0]):
        x = _layer(x, mem, cosf, sinf, {k: v[layer] for k, v in stacked.items()})
    b, s, d = x.shape
    return _final_norm(x.reshape(b * s, d), ln_final, tm=1024).reshape(b, s, d)
```

```python
import functools
import math

import jax
import jax.numpy as jnp
from jax import lax
from jax.experimental import pallas as pl
from jax.experimental.pallas import tpu as pltpu

F32 = jnp.float32
BF16 = jnp.bfloat16

D_MODEL = 2048
DEPTH = 4
CHUNK = 64
N_MEM = 256
GROUP_WIDTH = 512
N_HEADS = 4
HEAD_DIM = 128
GLA_DK = 64
GLA_LOWRANK = 16
GLA_NORMALIZER = 16.0
DN_CONV = 4
ROPE_THETA = 10000.0
N_CROSS_HEADS = 4
CROSS_HEAD_DIM = 512
D_FF = 4 * D_MODEL
NORM_EPS = 1e-6

N_MAIN = 14 * GROUP_WIDTH
N_SMALL = 128
LANE_BETA = GLA_LOWRANK
LANE_DECAY = GLA_LOWRANK + N_HEADS

SEQ_TILE = 256
MIB = 1024 * 1024


def _cparams(sem, vmem_mib):
    return pltpu.CompilerParams(dimension_semantics=sem, vmem_limit_bytes=vmem_mib * MIB)


def _dot(a, b):
    return jnp.dot(a, b, preferred_element_type=F32)


def _dot_nt(a, b):
    return lax.dot_general(a, b, (((1,), (1,)), ((), ())), preferred_element_type=F32)


def _dot_tn(a, b):
    return lax.dot_general(a, b, (((0,), (0,)), ((), ())), preferred_element_type=F32)


def _split2(x):
    hi = x.astype(BF16)
    lo = (x - hi.astype(F32)).astype(BF16)
    return hi, lo


def _split3(x):
    hi = x.astype(BF16)
    r = x - hi.astype(F32)
    mid = r.astype(BF16)
    lo = (r - mid.astype(F32)).astype(BF16)
    return hi, mid, lo


def _dot_exact_lhs(m_bf16, x):
    hi, mid, lo = _split3(x)
    return _dot(m_bf16, hi) + _dot(m_bf16, mid) + _dot(m_bf16, lo)


def _dot3(a, b):
    ah, al = _split2(a)
    bh, bl = _split2(b)
    return _dot(ah, bh) + (_dot(ah, bl) + _dot(al, bh))


def _sigmoid(x):
    return 1.0 / (1.0 + jnp.exp(-x))


def _silu(x):
    return x * _sigmoid(x)


def _softplus(x):
    return jnp.maximum(x, 0.0) + jnp.log1p(jnp.exp(-jnp.abs(x)))


def _log_sigmoid(x):
    return -_softplus(-x)


def _head_norm_gate(o, gain, gate):
    ms = jnp.mean(o * o, axis=-1, keepdims=True)
    return o * lax.rsqrt(ms + NORM_EPS) * gain * gate


def _matmul(name, a_parts, w, *, gain=None, res=None, col_scale=None, w_small=None, out_dtype, tm, tn, vmem_mib):
    m = a_parts[0].shape[0]
    k = sum(p.shape[1] for p in a_parts)
    n = w.shape[1]
    tm = min(tm, m)
    tn = min(tn, n)
    assert m % tm == 0 and n % tn == 0 and w.shape[0] == k
    norm = gain is not None
    n_parts = len(a_parts)
    assert not (norm and n_parts != 1)
    rows = min(256, tm)

    def kern(*refs):
        it = iter(refs)
        a_refs = [next(it) for _ in range(n_parts)]
        g_ref = next(it) if norm else None
        w_ref = next(it)
        ws_ref = next(it) if w_small is not None else None
        cs_ref = next(it) if col_scale is not None else None
        r_ref = next(it) if res is not None else None
        o_ref = next(it)
        os_ref = next(it) if w_small is not None else None
        h_ref = next(it) if norm else None

        if norm:
            @pl.when(pl.program_id(1) == 0)
            def _():
                _rms_rows(a_refs[0], h_ref, g_ref[...], rows, tm, BF16)
                if w_small is not None:
                    os_ref[...] = _dot(h_ref[...], ws_ref[...])
            lhs = h_ref[...]
        elif n_parts == 1:
            lhs = a_refs[0][...]
        else:
            lhs = jnp.concatenate([r[...] for r in a_refs], axis=-1)
        acc = _dot(lhs, w_ref[...])
        if col_scale is not None:
            acc = acc * cs_ref[...]
        if res is not None:
            acc = acc + r_ref[...]
        o_ref[...] = acc.astype(out_dtype)

    in_specs = [pl.BlockSpec((tm, p.shape[1]), lambda i, j: (i, 0)) for p in a_parts]
    args = list(a_parts)
    if norm:
        in_specs.append(pl.BlockSpec((1, k), lambda i, j: (0, 0)))
        args.append(gain.reshape(1, k))
    in_specs.append(pl.BlockSpec((k, tn), lambda i, j: (0, j)))
    args.append(w)
    if w_small is not None:
        in_specs.append(pl.BlockSpec((k, N_SMALL), lambda i, j: (0, 0)))
        args.append(w_small)
    if col_scale is not None:
        in_specs.append(pl.BlockSpec((1, tn), lambda i, j: (0, j)))
        args.append(col_scale.reshape(1, n))
    if res is not None:
        in_specs.append(pl.BlockSpec((tm, tn), lambda i, j: (i, j)))
        args.append(res)
    out_shape = [jax.ShapeDtypeStruct((m, n), out_dtype)]
    out_specs = [pl.BlockSpec((tm, tn), lambda i, j: (i, j))]
    if w_small is not None:
        out_shape.append(jax.ShapeDtypeStruct((m, N_SMALL), F32))
        out_specs.append(pl.BlockSpec((tm, N_SMALL), lambda i, j: (i, 0)))
    scratch = [pltpu.VMEM((tm, k), BF16)] if norm else []
    outs = pl.pallas_call(
        kern, grid=(m // tm, n // tn), in_specs=in_specs, out_specs=out_specs, out_shape=out_shape,
        scratch_shapes=scratch, compiler_params=_cparams(("parallel", "arbitrary"), vmem_mib), name=name,
    )(*args)
    return outs if w_small is not None else outs[0]


def _rms_rows(ref_in, ref_out, gain_row, rows, total, out_dtype):
    def body(r, carry):
        sl = pl.ds(pl.multiple_of(r * rows, rows), rows)
        xf = ref_in[sl, :]
        ms = jnp.mean(xf * xf, axis=-1, keepdims=True)
        ref_out[sl, :] = (xf * lax.rsqrt(ms + NORM_EPS) * gain_row).astype(out_dtype)
        return carry
    lax.fori_loop(0, total // rows, body, 0)


def _mlp(x, gain, w_up, w_down, final_gain, *, tm, tf, vmem_mib):
    m, d = x.shape
    f = w_up.shape[1]
    tm = min(tm, m)
    n_f = f // tf
    rows = min(256, tm)
    final = final_gain is not None

    def kern(*refs):
        if final:
            x_ref, g_ref, wu_ref, wd_ref, fg_ref, o_ref, h_ref = refs
        else:
            x_ref, g_ref, wu_ref, wd_ref, o_ref, h_ref = refs
        step = pl.program_id(1)

        @pl.when(step == 0)
        def _():
            _rms_rows(x_ref, h_ref, g_ref[...], rows, tm, BF16)
            o_ref[...] = x_ref[...]

        u = jnp.maximum(_dot(h_ref[...], wu_ref[...]), 0.0)
        o_ref[...] += _dot((u * u).astype(BF16), wd_ref[...])

        if final:
            @pl.when(step == n_f - 1)
            def _():
                _rms_rows(o_ref, o_ref, fg_ref[...], rows, tm, F32)

    in_specs = [pl.BlockSpec((tm, d), lambda i, j: (i, 0)), pl.BlockSpec((1, d), lambda i, j: (0, 0)),
                pl.BlockSpec((d, tf), lambda i, j: (0, j)), pl.BlockSpec((tf, d), lambda i, j: (j, 0))]
    args = [x, gain.reshape(1, d), w_up, w_down]
    if final:
        in_specs.append(pl.BlockSpec((1, d), lambda i, j: (0, 0)))
        args.append(final_gain.reshape(1, d))
    return pl.pallas_call(
        kern, grid=(m // tm, n_f), in_specs=in_specs,
        out_specs=pl.BlockSpec((tm, d), lambda i, j: (i, 0)),
        out_shape=jax.ShapeDtypeStruct((m, d), F32),
        scratch_shapes=[pltpu.VMEM((tm, d), BF16)],
        compiler_params=_cparams(("parallel", "arbitrary"), vmem_mib), name="mlp",
    )(*args)


def _rotary_tables(positions):
    b, s = positions.shape
    inv_freq = ROPE_THETA ** (-jnp.arange(0, HEAD_DIM, 2, dtype=F32) / HEAD_DIM)
    inv2 = jnp.concatenate([inv_freq, inv_freq]).reshape(1, HEAD_DIM)
    sign = jnp.concatenate([-jnp.ones((HEAD_DIM // 2,), F32), jnp.ones((HEAD_DIM // 2,), F32)]).reshape(1, HEAD_DIM)

    def kern(p_ref, f_ref, s_ref, cos_ref, sin_ref):
        ang = p_ref[0].astype(F32) * f_ref[...]
        cos_ref[0] = jnp.cos(ang)
        sin_ref[0] = jnp.sin(ang) * s_ref[...]

    tbl = jax.ShapeDtypeStruct((b, s, HEAD_DIM), F32)
    return pl.pallas_call(
        kern, grid=(b,),
        in_specs=[pl.BlockSpec((1, s, 1), lambda i: (i, 0, 0)),
                  pl.BlockSpec((1, HEAD_DIM), lambda i: (0, 0)),
                  pl.BlockSpec((1, HEAD_DIM), lambda i: (0, 0))],
        out_specs=[pl.BlockSpec((1, s, HEAD_DIM), lambda i: (i, 0, 0))] * 2,
        out_shape=[tbl, tbl],
        compiler_params=_cparams(("parallel",), 32), name="rotary_tables",
    )(positions.reshape(b, s, 1), inv2, sign)


def _proj_spec(col_block):
    return pl.BlockSpec((1, SEQ_TILE, GROUP_WIDTH), lambda b, s: (b, s, col_block))


def _const_spec(shape):
    nd = len(shape)
    return pl.BlockSpec(shape, lambda b, s: (0,) * nd)


def _chunk_masks():
    t = jnp.arange(SEQ_TILE)
    same = (t[:, None] // CHUNK) == (t[None, :] // CHUNK)
    incl = (same & (t[None, :] <= t[:, None])).astype(F32)
    strict = (same & (t[None, :] < t[:, None])).astype(F32)
    return incl, strict


def _retention(proj, cosf, sinf, gain):
    b, s, _ = proj.shape
    t = SEQ_TILE
    f32 = F32
    log_gamma = jnp.log1p(-jnp.exp2(-5.0 - jnp.arange(N_HEADS, dtype=f32)))
    pos = jnp.arange(t, dtype=f32)
    cpos = jnp.arange(t) % CHUNK
    scale = HEAD_DIM ** -0.5
    dist = jnp.abs(pos[:, None] - pos[None, :])
    allowed = (pos[None, :] <= pos[:, None]) | ((jnp.arange(t)[:, None] // CHUNK) == (jnp.arange(t)[None, :] // CHUNK))
    mask = jnp.where(allowed[None], jnp.exp(log_gamma[:, None, None] * dist[None]), 0.0) * scale
    qdec = jnp.exp(log_gamma[:, None] * (pos + 1.0))[..., None] * scale
    kdec = jnp.exp(log_gamma[:, None] * (t - 1.0 - pos))[..., None]
    cdec = jnp.exp(log_gamma * t)[:, None, None]
    qdec = jnp.broadcast_to(qdec, (N_HEADS, t, HEAD_DIM))
    kdec = jnp.broadcast_to(kdec, (N_HEADS, t, HEAD_DIM))
    cdec = jnp.broadcast_to(cdec, (N_HEADS, 1, HEAD_DIM))
    del cpos

    def kern(q_ref, k_ref, v_ref, g_ref, cos_ref, sin_ref, m_ref, qd_ref, kd_ref, cd_ref, gain_ref, o_ref, st_ref):
        @pl.when(pl.program_id(1) == 0)
        def _():
            st_ref[...] = jnp.zeros_like(st_ref)

        cosv = cos_ref[0]
        sinv = sin_ref[0]
        for h in range(N_HEADS):
            sl = slice(h * HEAD_DIM, (h + 1) * HEAD_DIM)
            q = q_ref[0, :, sl].astype(F32)
            k = k_ref[0, :, sl].astype(F32)
            v = v_ref[0, :, sl]
            qr = q * cosv + pltpu.roll(q, HEAD_DIM // 2, 1) * sinv
            kr = k * cosv + pltpu.roll(k, HEAD_DIM // 2, 1) * sinv
            sc = _dot_nt(qr.astype(BF16), kr.astype(BF16)) * m_ref[h]
            st = st_ref[h]
            o = _dot(sc.astype(BF16), v) + _dot((qr * qd_ref[h]).astype(BF16), st.astype(BF16))
            st_ref[h] = cd_ref[h] * st + _dot_tn((kr * kd_ref[h]).astype(BF16), v)
            gate = _silu(g_ref[0, :, sl].astype(F32))
            o_ref[0, :, sl] = _head_norm_gate(o, gain_ref[:, sl], gate).astype(BF16)

    return pl.pallas_call(
        kern, grid=(b, s // t),
        in_specs=[_proj_spec(0), _proj_spec(1), _proj_spec(2), _proj_spec(3),
                  pl.BlockSpec((1, t, HEAD_DIM), lambda i, j: (i, j, 0)),
                  pl.BlockSpec((1, t, HEAD_DIM), lambda i, j: (i, j, 0)),
                  _const_spec((N_HEADS, t, t)), _const_spec((N_HEADS, t, HEAD_DIM)),
                  _const_spec((N_HEADS, t, HEAD_DIM)), _const_spec((N_HEADS, 1, HEAD_DIM)),
                  _const_spec((1, GROUP_WIDTH))],
        out_specs=pl.BlockSpec((1, t, GROUP_WIDTH), lambda i, j: (i, j, 0)),
        out_shape=jax.ShapeDtypeStruct((b, s, GROUP_WIDTH), BF16),
        scratch_shapes=[pltpu.VMEM((N_HEADS, HEAD_DIM, HEAD_DIM), F32)],
        compiler_params=_cparams(("parallel", "arbitrary"), 32), name="retention",
    )(proj, proj, proj, proj, cosf, sinf, mask, qdec, kdec, cdec, gain.reshape(1, GROUP_WIDTH))


def _gla(proj, small, w_up, b_up, gain):
    b, s, _ = proj.shape
    t = SEQ_TILE
    n_chunks = t // CHUNK
    incl, _ = _chunk_masks()
    w_up_pad = jnp.zeros((N_SMALL, N_HEADS * GLA_DK), F32).at[:GLA_LOWRANK].set(w_up)
    scale = GLA_DK ** -0.5

    def kern(qk_ref, v_ref, g_ref, sm_ref, wup_ref, bup_ref, tri_ref, gain_ref, o_ref, st_ref):
        @pl.when(pl.program_id(1) == 0)
        def _():
            st_ref[...] = jnp.zeros_like(st_ref)

        logit = _dot3(sm_ref[0], wup_ref[...]) + bup_ref[...]
        lg = _log_sigmoid(logit) * (1.0 / GLA_NORMALIZER)
        tri = tri_ref[...]
        bcum = _dot_exact_lhs(tri.astype(BF16), lg)
        e_pos = jnp.exp(bcum)
        e_neg = jnp.exp(-bcum)
        hr = range(N_HEADS)
        dsls = [slice(h * GLA_DK, (h + 1) * GLA_DK) for h in hr]
        vsls = [slice(h * HEAD_DIM, (h + 1) * HEAD_DIM) for h in hr]
        q_all = qk_ref[0, :, :N_HEADS * GLA_DK].astype(F32) * scale
        k_all = qk_ref[0, :, N_HEADS * GLA_DK:].astype(F32)
        qt_all = (q_all * e_pos).astype(BF16)
        kt_all = (k_all * e_neg).astype(BF16)
        scores = [(_dot_nt(qt_all[:, dsls[h]], kt_all[:, dsls[h]]) * tri).astype(BF16) for h in hr]
        o_intra = [_dot(scores[h], v_ref[0, :, vsls[h]]) for h in hr]
        sts = [st_ref[h] for h in hr]
        inter = [[] for _ in hr]
        for c in range(n_chunks):
            rows = slice(c * CHUNK, (c + 1) * CHUNK)
            b_last = bcum[(c + 1) * CHUNK - 1:(c + 1) * CHUNK, :]
            kd_all = (k_all[rows] * jnp.exp(b_last - bcum[rows])).astype(BF16)
            e_last = jnp.exp(b_last)
            for h in hr:
                inter[h].append(_dot_nt(qt_all[rows, dsls[h]], sts[h].astype(BF16)))
                sts[h] = sts[h] * e_last[:, dsls[h]] + _dot_tn(v_ref[0, rows, vsls[h]], kd_all[:, dsls[h]])
        for h in hr:
            st_ref[h] = sts[h]
            o = o_intra[h] + jnp.concatenate(inter[h], axis=0)
            gate = _silu(g_ref[0, :, vsls[h]].astype(F32))
            o_ref[0, :, vsls[h]] = _head_norm_gate(o, gain_ref[:, vsls[h]], gate).astype(BF16)

    return pl.pallas_call(
        kern, grid=(b, s // t),
        in_specs=[_proj_spec(4), _proj_spec(5), _proj_spec(6),
                  pl.BlockSpec((1, t, N_SMALL), lambda i, j: (i, j, 0)),
                  _const_spec((N_SMALL, N_HEADS * GLA_DK)), _const_spec((1, N_HEADS * GLA_DK)),
                  _const_spec((t, t)), _const_spec((1, GROUP_WIDTH))],
        out_specs=pl.BlockSpec((1, t, GROUP_WIDTH), lambda i, j: (i, j, 0)),
        out_shape=jax.ShapeDtypeStruct((b, s, GROUP_WIDTH), BF16),
        scratch_shapes=[pltpu.VMEM((N_HEADS, HEAD_DIM, GLA_DK), F32)],
        compiler_params=_cparams(("parallel", "arbitrary"), 32), name="gla",
    )(proj, proj, proj, small, w_up_pad, b_up.reshape(1, -1), incl, gain.reshape(1, GROUP_WIDTH))


def _stick_breaking(proj, gain):
    b, s, _ = proj.shape
    t = SEQ_TILE
    idx = jnp.arange(t)
    later = (idx[:, None] > idx[None, :]).astype(BF16)
    causal = (idx[None, :] < idx[:, None]).astype(F32)

    def kern(q_ref, k_ref, v_ref, later_ref, causal_ref, gain_ref, o_ref, acc_ref, carry_ref):
        qi = pl.program_id(1)

        def key_block(row, diag):
            heads = [slice(h * HEAD_DIM, (h + 1) * HEAD_DIM) for h in range(N_HEADS)]
            zs = [_dot_nt(q_ref[0, :, sl], k_ref[0, pl.ds(row, t), sl]) for sl in heads]
            log_betas, sums = [], []
            for h in range(N_HEADS):
                z = zs[h]
                sp = jnp.log(1.0 + jnp.exp(-jnp.abs(z)))
                log_beta = jnp.minimum(z, 0.0) - sp
                log_keep = log_beta - z
                if diag:
                    log_keep = log_keep * causal_ref[...]
                log_later = _dot(log_keep.astype(BF16), later_ref[...])
                if not diag:
                    log_later = log_later + carry_ref[h]
                log_betas.append(log_beta + log_later)
                row_sum = jnp.sum(log_keep, axis=-1, keepdims=True)
                if diag:
                    carry_ref[h] = row_sum
                else:
                    carry_ref[h] += row_sum
            for h in range(N_HEADS):
                w = jnp.exp(log_betas[h])
                if diag:
                    w = w * causal_ref[...]
                pv = _dot(w.astype(BF16), v_ref[0, pl.ds(row, t), heads[h]])
                if diag:
                    acc_ref[h] = pv
                else:
                    acc_ref[h] += pv

        key_block(pl.multiple_of(qi * t, t), True)

        def body(i, c):
            key_block(pl.multiple_of((qi - 1 - i) * t, t), False)
            return c

        lax.fori_loop(0, qi, body, 0)
        for h in range(N_HEADS):
            sl = slice(h * HEAD_DIM, (h + 1) * HEAD_DIM)
            o_ref[0, :, sl] = _head_norm_gate(acc_ref[h], gain_ref[:, sl], 1.0).astype(BF16)

    return pl.pallas_call(
        kern, grid=(b, s // t),
        in_specs=[_proj_spec(7),
                  pl.BlockSpec((1, s, GROUP_WIDTH), lambda i, j: (i, 0, 8)),
                  pl.BlockSpec((1, s, GROUP_WIDTH), lambda i, j: (i, 0, 9)),
                  _const_spec((t, t)), _const_spec((t, t)), _const_spec((1, GROUP_WIDTH))],
        out_specs=pl.BlockSpec((1, t, GROUP_WIDTH), lambda i, j: (i, j, 0)),
        out_shape=jax.ShapeDtypeStruct((b, s, GROUP_WIDTH), BF16),
        scratch_shapes=[pltpu.VMEM((N_HEADS, t, HEAD_DIM), F32), pltpu.VMEM((N_HEADS, t, 1), F32)],
        compiler_params=_cparams(("parallel", "arbitrary"), 40), name="stick_breaking",
    )(proj, proj, proj, later, causal, gain.reshape(1, GROUP_WIDTH))


def _deltanet(proj, small, conv_w, a_log, dt_bias, gain):
    b, s, _ = proj.shape
    t = SEQ_TILE
    n_chunks = t // CHUNK
    incl, strict = _chunk_masks()
    a_pad = jnp.zeros((1, N_SMALL), F32).at[0, LANE_DECAY:LANE_DECAY + N_HEADS].set(a_log)
    dt_pad = jnp.zeros((1, N_SMALL), F32).at[0, LANE_DECAY:LANE_DECAY + N_HEADS].set(dt_bias)
    sel = jnp.zeros((8, N_SMALL), F32).at[jnp.arange(N_HEADS), LANE_DECAY + jnp.arange(N_HEADS)].set(1.0).astype(BF16)
    qscale = HEAD_DIM ** -0.5
    pad = 8

    def kern(q_ref, k_ref, v_ref, z_ref, sm_ref, cw_ref, alog_ref, dtb_ref, incl_ref, strict_ref, sel_ref, gain_ref,
             o_ref, st_ref, tail_ref):
        @pl.when(pl.program_id(1) == 0)
        def _():
            st_ref[...] = jnp.zeros_like(st_ref)
            tail_ref[...] = jnp.zeros_like(tail_ref)

        convd = []
        for i, ref in enumerate((q_ref, k_ref, v_ref)):
            x = ref[0].astype(F32)
            full = jnp.concatenate([tail_ref[i], x], axis=0)
            csl = slice(i * GROUP_WIDTH, (i + 1) * GROUP_WIDTH)
            y = full[pad:pad + t] * cw_ref[DN_CONV - 1:DN_CONV, csl]
            for tap in range(DN_CONV - 1):
                off = pad - (DN_CONV - 1) + tap
                y = y + full[off:off + t] * cw_ref[tap:tap + 1, csl]
            tail_ref[i] = x[t - pad:]
            convd.append(_silu(y))
        cq, ck, cv = convd

        sm = sm_ref[0]
        beta_all = _sigmoid(sm)
        g_all = -jnp.exp(alog_ref[...]) * _softplus(sm + dtb_ref[...])
        incl_m = incl_ref[...]
        strict_m = strict_ref[...]
        gcum = _dot_exact_lhs(incl_m.astype(BF16), g_all)
        ghi, gmid, glo = _split3(gcum)
        sel_m = sel_ref[...]
        grow = _dot_nt(sel_m, ghi) + _dot_nt(sel_m, gmid) + _dot_nt(sel_m, glo)
        egc = jnp.exp(gcum)

        hr = range(N_HEADS)
        sls = [slice(h * HEAD_DIM, (h + 1) * HEAD_DIM) for h in hr]
        gcs = [gcum[:, LANE_DECAY + h:LANE_DECAY + h + 1] for h in hr]
        egs = [egc[:, LANE_DECAY + h:LANE_DECAY + h + 1] for h in hr]
        betas = [beta_all[:, LANE_BETA + h:LANE_BETA + h + 1] for h in hr]
        qs, ks, k16s, kbs, decays, n1s, scores = [], [], [], [], [], [], []
        for h in hr:
            q = cq[:, sls[h]]
            k = ck[:, sls[h]]
            q = q * lax.rsqrt(jnp.sum(q * q, axis=-1, keepdims=True) + 1e-6) * qscale
            k = k * lax.rsqrt(jnp.sum(k * k, axis=-1, keepdims=True) + 1e-6)
            diff = gcs[h] - grow[h:h + 1, :]
            decay = jnp.exp(jnp.minimum(diff, 0.0)) * incl_m
            kb = k * betas[h]
            k16 = k.astype(BF16)
            n1s.append(-(_dot_nt(kb.astype(BF16), k16) * decay * strict_m))
            scores.append((_dot_nt(q.astype(BF16), k16) * decay).astype(BF16))
            qs.append(q); ks.append(k); k16s.append(k16); kbs.append(kb)
        n16s = [n.astype(BF16) for n in n1s]
        a_s = [_dot(n, n) for n in n16s]
        p_s = list(n1s)
        levels = int(math.log2(CHUNK)) - 1
        for lvl in range(levels):
            for h in hr:
                a16 = a_s[h].astype(BF16)
                if lvl < levels - 1:
                    prod = _dot(jnp.concatenate([p_s[h].astype(BF16), a16], axis=0), a16)
                    p_s[h] = p_s[h] + a_s[h] + prod[:t]
                    a_s[h] = prod[t:]
                else:
                    p_s[h] = p_s[h] + a_s[h] + _dot(p_s[h].astype(BF16), a16)
        us, w16s, qes = [], [], []
        for h in hr:
            rhs = jnp.concatenate([cv[:, sls[h]] * betas[h], kbs[h] * egs[h]], axis=-1)
            sol = rhs + _dot(p_s[h].astype(BF16), rhs.astype(BF16))
            us.append(sol[:, :HEAD_DIM])
            w16s.append(sol[:, HEAD_DIM:].astype(BF16))
            qes.append((qs[h] * egs[h]).astype(BF16))
        sts = [st_ref[h] for h in hr]
        v_new = [[] for _ in hr]
        inter = [[] for _ in hr]
        for c in range(n_chunks):
            rows = slice(c * CHUNK, (c + 1) * CHUNK)
            last = slice((c + 1) * CHUNK - 1, (c + 1) * CHUNK)
            for h in hr:
                both = _dot(jnp.concatenate([w16s[h][rows], qes[h][rows]], axis=0), sts[h].astype(BF16))
                vn = us[h][rows] - both[:CHUNK]
                inter[h].append(both[CHUNK:])
                g_last = gcs[h][last, :]
                kd = (ks[h][rows] * jnp.exp(g_last - gcs[h][rows])).astype(BF16)
                sts[h] = jnp.exp(g_last) * sts[h] + _dot_tn(kd, vn.astype(BF16))
                v_new[h].append(vn)
        for h in hr:
            st_ref[h] = sts[h]
            vn_all = jnp.concatenate(v_new[h], axis=0).astype(BF16)
            o = jnp.concatenate(inter[h], axis=0) + _dot(scores[h], vn_all)
            gate = _silu(z_ref[0, :, sls[h]].astype(F32))
            o_ref[0, :, sls[h]] = _head_norm_gate(o, gain_ref[:, sls[h]], gate).astype(BF16)

    return pl.pallas_call(
        kern, grid=(b, s // t),
        in_specs=[_proj_spec(10), _proj_spec(11), _proj_spec(12), _proj_spec(13),
                  pl.BlockSpec((1, t, N_SMALL), lambda i, j: (i, j, 0)),
                  _const_spec((DN_CONV, 3 * GROUP_WIDTH)), _const_spec((1, N_SMALL)), _const_spec((1, N_SMALL)),
                  _const_spec((t, t)), _const_spec((t, t)), _const_spec((8, N_SMALL)), _const_spec((1, GROUP_WIDTH))],
        out_specs=pl.BlockSpec((1, t, GROUP_WIDTH), lambda i, j: (i, j, 0)),
        out_shape=jax.ShapeDtypeStruct((b, s, GROUP_WIDTH), BF16),
        scratch_shapes=[pltpu.VMEM((N_HEADS, HEAD_DIM, HEAD_DIM), F32), pltpu.VMEM((3, pad, GROUP_WIDTH), F32)],
        compiler_params=_cparams(("parallel", "arbitrary"), 40), name="deltanet",
    )(proj, proj, proj, proj, small, conv_w, a_pad, dt_pad, incl, strict, sel, gain.reshape(1, GROUP_WIDTH))


def _cross_attention(q, kv, *, tq):
    b, s, d = q.shape
    tq = min(tq, s)
    scale = CROSS_HEAD_DIM ** -0.5

    def kern(q_ref, kv_ref, o_ref):
        for h in range(N_CROSS_HEADS):
            sl = slice(h * CROSS_HEAD_DIM, (h + 1) * CROSS_HEAD_DIM)
            vsl = slice(d + h * CROSS_HEAD_DIM, d + (h + 1) * CROSS_HEAD_DIM)
            sc = _dot_nt(q_ref[0, :, sl], kv_ref[0, :, sl]) * scale
            mx = jnp.max(sc, axis=-1, keepdims=True)
            p = jnp.exp(sc - mx)
            denom = jnp.sum(p, axis=-1, keepdims=True)
            o = _dot(p.astype(BF16), kv_ref[0, :, vsl]) / denom
            o_ref[0, :, sl] = o.astype(BF16)

    return pl.pallas_call(
        kern, grid=(b, s // tq),
        in_specs=[pl.BlockSpec((1, tq, d), lambda i, j: (i, j, 0)),
                  pl.BlockSpec((1, N_MEM, 2 * d), lambda i, j: (i, 0, 0))],
        out_specs=pl.BlockSpec((1, tq, d), lambda i, j: (i, j, 0)),
        out_shape=jax.ShapeDtypeStruct((b, s, d), BF16),
        compiler_params=_cparams(("parallel", "arbitrary"), 40), name="cross_attention",
    )(q, kv)


def _reorder_w_in(w_in):
    lr0 = 4 * GROUP_WIDTH + 2 * N_HEADS * GLA_DK + 2 * GROUP_WIDTH
    sb0 = lr0 + GLA_LOWRANK
    db0 = sb0 + 3 * GROUP_WIDTH + 3 * GROUP_WIDTH + GROUP_WIDTH
    main = jnp.concatenate([w_in[:, :lr0], w_in[:, sb0:db0]], axis=1).astype(BF16)
    small = jnp.concatenate(
        [w_in[:, lr0:sb0], w_in[:, db0:db0 + 2 * N_HEADS],
         jnp.zeros((w_in.shape[0], N_SMALL - GLA_LOWRANK - 2 * N_HEADS), w_in.dtype)], axis=1).astype(BF16)
    return main, small


def _layer(x2, mem2, b, s, cosf, sinf, p, final_gain):
    m, d = x2.shape
    w_main, w_small = _reorder_w_in(p["w_in"])
    col_scale = jnp.ones((N_MAIN,), F32).at[7 * GROUP_WIDTH:8 * GROUP_WIDTH].set(HEAD_DIM ** -0.5)
    proj, small = _matmul("in_proj", [x2], w_main, gain=p["ln_mix"], w_small=w_small, col_scale=col_scale,
                          out_dtype=BF16, tm=1024, tn=1024, vmem_mib=48)
    proj = proj.reshape(b, s, N_MAIN)
    small = small.reshape(b, s, N_SMALL)
    gains = jnp.split(p["mix_head_norm"], 4)
    y_ret = _retention(proj, cosf, sinf, gains[0])
    y_gla = _gla(proj, small, p["gla_w_up"], p["gla_b_up"], gains[1])
    y_sb = _stick_breaking(proj, gains[2])
    y_dn = _deltanet(proj, small, p["dn_conv_w"], p["dn_a_log"], p["dn_dt_bias"], gains[3])
    parts = [y.reshape(m, GROUP_WIDTH) for y in (y_ret, y_gla, y_sb, y_dn)]
    x2 = _matmul("mix_out", parts, p["w_mix_out"].astype(BF16), res=x2, out_dtype=F32, tm=1024, tn=1024, vmem_mib=48)

    q = _matmul("cross_q", [x2], p["w_cross_q"].astype(BF16), gain=p["ln_cross"], out_dtype=BF16,
                tm=1024, tn=1024, vmem_mib=48)
    kv = _matmul("cross_kv", [mem2], p["w_cross_kv"].astype(BF16), gain=p["ln_mem"], out_dtype=BF16,
                 tm=1024, tn=1024, vmem_mib=48)
    o = _cross_attention(q.reshape(b, s, d), kv.reshape(b, N_MEM, 2 * d), tq=1024)
    x2 = _matmul("cross_o", [o.reshape(m, d)], p["w_cross_o"].astype(BF16), res=x2, out_dtype=F32,
                 tm=1024, tn=1024, vmem_mib=48)
    return _mlp(x2, p["ln_mlp"], p["w_mlp_up"].astype(BF16), p["w_mlp_down"].astype(BF16), final_gain,
                tm=1024, tf=512, vmem_mib=56)


def kernel(x, mem, positions, ln_mix, w_in, gla_w_up, gla_b_up, dn_conv_w, dn_a_log, dn_dt_bias, mix_head_norm,
           w_mix_out, ln_cross, ln_mem, w_cross_q, w_cross_kv, w_cross_o, ln_mlp, w_mlp_up, w_mlp_down, ln_final):
    cosf, sinf = _rotary_tables(positions)
    stacked = dict(ln_mix=ln_mix, w_in=w_in, gla_w_up=gla_w_up, gla_b_up=gla_b_up, dn_conv_w=dn_conv_w,
                   dn_a_log=dn_a_log, dn_dt_bias=dn_dt_bias, mix_head_norm=mix_head_norm, w_mix_out=w_mix_out,
                   ln_cross=ln_cross, ln_mem=ln_mem, w_cross_q=w_cross_q, w_cross_kv=w_cross_kv,
                   w_cross_o=w_cross_o, ln_mlp=ln_mlp, w_mlp_up=w_mlp_up, w_mlp_down=w_mlp_down)
    b, s, d = x.shape
    depth = ln_mix.shape[0]
    x2 = x.reshape(b * s, d)
    mem2 = mem.reshape(-1, d)
    for layer in range(depth):
        x2 = _layer(x2, mem2, b, s, cosf, sinf, {k: v[layer] for k, v in stacked.items()},
                    ln_final if layer == depth - 1 else None)
    return x2.reshape(b, s, d)
```

```python
import functools
import math

import jax
import jax.numpy as jnp
from jax import lax
from jax.experimental import pallas as pl
from jax.experimental.pallas import tpu as pltpu

F32 = jnp.float32
BF16 = jnp.bfloat16

D_MODEL = 2048
DEPTH = 4
CHUNK = 64
N_MEM = 256
GROUP_WIDTH = 512
N_HEADS = 4
HEAD_DIM = 128
GLA_DK = 64
GLA_LOWRANK = 16
GLA_NORMALIZER = 16.0
DN_CONV = 4
ROPE_THETA = 10000.0
N_CROSS_HEADS = 4
CROSS_HEAD_DIM = 512
D_FF = 4 * D_MODEL
NORM_EPS = 1e-6

N_MAIN = 14 * GROUP_WIDTH
N_SMALL = 128
LANE_BETA = GLA_LOWRANK
LANE_DECAY = GLA_LOWRANK + N_HEADS

SEQ_TILE = 256
MIB = 1024 * 1024


def _cparams(sem, vmem_mib):
    return pltpu.CompilerParams(dimension_semantics=sem, vmem_limit_bytes=vmem_mib * MIB)


def _dot(a, b):
    return jnp.dot(a, b, preferred_element_type=F32)


def _dot_nt(a, b):
    return lax.dot_general(a, b, (((1,), (1,)), ((), ())), preferred_element_type=F32)


def _dot_tn(a, b):
    return lax.dot_general(a, b, (((0,), (0,)), ((), ())), preferred_element_type=F32)


def _split2(x):
    hi = x.astype(BF16)
    lo = (x - hi.astype(F32)).astype(BF16)
    return hi, lo


def _split3(x):
    hi = x.astype(BF16)
    r = x - hi.astype(F32)
    mid = r.astype(BF16)
    lo = (r - mid.astype(F32)).astype(BF16)
    return hi, mid, lo


def _dot_exact_lhs(m_bf16, x):
    hi, mid, lo = _split3(x)
    return _dot(m_bf16, hi) + _dot(m_bf16, mid) + _dot(m_bf16, lo)


def _dot3(a, b):
    ah, al = _split2(a)
    bh, bl = _split2(b)
    return _dot(ah, bh) + (_dot(ah, bl) + _dot(al, bh))


def _sigmoid(x):
    return 1.0 / (1.0 + jnp.exp(-x))


def _silu(x):
    return x * _sigmoid(x)


def _softplus(x):
    return jnp.maximum(x, 0.0) + jnp.log1p(jnp.exp(-jnp.abs(x)))


def _log_sigmoid(x):
    return -_softplus(-x)


def _head_norm_gate(o, gain, gate):
    ms = jnp.mean(o * o, axis=-1, keepdims=True)
    return o * lax.rsqrt(ms + NORM_EPS) * gain * gate


def _matmul(name, a_parts, w, layer, *, gain=None, res=None, col_scale=None, w_small=None, out_dtype, tm, tn,
            vmem_mib):
    m = a_parts[0].shape[0]
    k = sum(p.shape[1] for p in a_parts)
    n = w.shape[2]
    tm = min(tm, m)
    tn = min(tn, n)
    assert m % tm == 0 and n % tn == 0 and w.shape[1] == k
    norm = gain is not None
    n_parts = len(a_parts)
    assert not (norm and n_parts != 1)
    rows = min(256, tm)

    def kern(*refs):
        it = iter(refs)
        a_refs = [next(it) for _ in range(n_parts)]
        g_ref = next(it) if norm else None
        w_ref = next(it)
        ws_ref = next(it) if w_small is not None else None
        cs_ref = next(it) if col_scale is not None else None
        r_ref = next(it) if res is not None else None
        o_ref = next(it)
        os_ref = next(it) if w_small is not None else None
        h_ref = next(it) if norm else None

        if norm:
            @pl.when(pl.program_id(1) == 0)
            def _():
                _rms_rows(a_refs[0], h_ref, g_ref[...], rows, tm, BF16)
                if w_small is not None:
                    os_ref[...] = _dot(h_ref[...], ws_ref[...])
            lhs = h_ref[...]
        elif n_parts == 1:
            lhs = a_refs[0][...]
        else:
            lhs = jnp.concatenate([r[...] for r in a_refs], axis=-1)
        acc = _dot(lhs, w_ref[...])
        if col_scale is not None:
            acc = acc * cs_ref[...]
        if res is not None:
            acc = acc + r_ref[...]
        o_ref[...] = acc.astype(out_dtype)

    in_specs = [pl.BlockSpec((tm, p.shape[1]), lambda i, j: (i, 0)) for p in a_parts]
    args = list(a_parts)
    if norm:
        in_specs.append(pl.BlockSpec((1, k), lambda i, j: (0, 0)))
        args.append(gain.reshape(1, k))
    in_specs.append(pl.BlockSpec((None, k, tn), lambda i, j: (layer, 0, j)))
    args.append(w)
    if w_small is not None:
        in_specs.append(pl.BlockSpec((None, k, N_SMALL), lambda i, j: (layer, 0, 0)))
        args.append(w_small)
    if col_scale is not None:
        in_specs.append(pl.BlockSpec((1, tn), lambda i, j: (0, j)))
        args.append(col_scale.reshape(1, n))
    if res is not None:
        in_specs.append(pl.BlockSpec((tm, tn), lambda i, j: (i, j)))
        args.append(res)
    out_shape = [jax.ShapeDtypeStruct((m, n), out_dtype)]
    out_specs = [pl.BlockSpec((tm, tn), lambda i, j: (i, j))]
    if w_small is not None:
        out_shape.append(jax.ShapeDtypeStruct((m, N_SMALL), F32))
        out_specs.append(pl.BlockSpec((tm, N_SMALL), lambda i, j: (i, 0)))
    scratch = [pltpu.VMEM((tm, k), BF16)] if norm else []
    outs = pl.pallas_call(
        kern, grid=(m // tm, n // tn), in_specs=in_specs, out_specs=out_specs, out_shape=out_shape,
        scratch_shapes=scratch, compiler_params=_cparams(("parallel", "arbitrary"), vmem_mib), name=name,
    )(*args)
    return outs if w_small is not None else outs[0]


def _rms_rows(ref_in, ref_out, gain_row, rows, total, out_dtype):
    def body(r, carry):
        sl = pl.ds(pl.multiple_of(r * rows, rows), rows)
        xf = ref_in[sl, :]
        ms = jnp.mean(xf * xf, axis=-1, keepdims=True)
        ref_out[sl, :] = (xf * lax.rsqrt(ms + NORM_EPS) * gain_row).astype(out_dtype)
        return carry
    lax.fori_loop(0, total // rows, body, 0)


def _mlp(x, gain, w_up, w_down, layer, final_gain, *, tm, tf, vmem_mib):
    m, d = x.shape
    f = w_up.shape[2]
    tm = min(tm, m)
    n_f = f // tf
    rows = min(256, tm)
    final = final_gain is not None

    def kern(*refs):
        if final:
            x_ref, g_ref, wu_ref, wd_ref, fg_ref, o_ref, h_ref = refs
        else:
            x_ref, g_ref, wu_ref, wd_ref, o_ref, h_ref = refs
        step = pl.program_id(1)

        @pl.when(step == 0)
        def _():
            _rms_rows(x_ref, h_ref, g_ref[...], rows, tm, BF16)
            o_ref[...] = x_ref[...]

        u = jnp.maximum(_dot(h_ref[...], wu_ref[...]), 0.0)
        o_ref[...] += _dot((u * u).astype(BF16), wd_ref[...])

        if final:
            @pl.when(step == n_f - 1)
            def _():
                _rms_rows(o_ref, o_ref, fg_ref[...], rows, tm, F32)

    in_specs = [pl.BlockSpec((tm, d), lambda i, j: (i, 0)), pl.BlockSpec((1, d), lambda i, j: (0, 0)),
                pl.BlockSpec((None, d, tf), lambda i, j: (layer, 0, j)),
                pl.BlockSpec((None, tf, d), lambda i, j: (layer, j, 0))]
    args = [x, gain.reshape(1, d), w_up, w_down]
    if final:
        in_specs.append(pl.BlockSpec((1, d), lambda i, j: (0, 0)))
        args.append(final_gain.reshape(1, d))
    return pl.pallas_call(
        kern, grid=(m // tm, n_f), in_specs=in_specs,
        out_specs=pl.BlockSpec((tm, d), lambda i, j: (i, 0)),
        out_shape=jax.ShapeDtypeStruct((m, d), F32),
        scratch_shapes=[pltpu.VMEM((tm, d), BF16)],
        compiler_params=_cparams(("parallel", "arbitrary"), vmem_mib), name="mlp",
    )(*args)


def _rotary_tables(positions):
    b, s = positions.shape
    inv_freq = ROPE_THETA ** (-jnp.arange(0, HEAD_DIM, 2, dtype=F32) / HEAD_DIM)
    inv2 = jnp.concatenate([inv_freq, inv_freq]).reshape(1, HEAD_DIM)
    sign = jnp.concatenate([-jnp.ones((HEAD_DIM // 2,), F32), jnp.ones((HEAD_DIM // 2,), F32)]).reshape(1, HEAD_DIM)

    def kern(p_ref, f_ref, s_ref, cos_ref, sin_ref):
        ang = p_ref[0].astype(F32) * f_ref[...]
        cos_ref[0] = jnp.cos(ang)
        sin_ref[0] = jnp.sin(ang) * s_ref[...]

    tbl = jax.ShapeDtypeStruct((b, s, HEAD_DIM), F32)
    return pl.pallas_call(
        kern, grid=(b,),
        in_specs=[pl.BlockSpec((1, s, 1), lambda i: (i, 0, 0)),
                  pl.BlockSpec((1, HEAD_DIM), lambda i: (0, 0)),
                  pl.BlockSpec((1, HEAD_DIM), lambda i: (0, 0))],
        out_specs=[pl.BlockSpec((1, s, HEAD_DIM), lambda i: (i, 0, 0))] * 2,
        out_shape=[tbl, tbl],
        compiler_params=_cparams(("parallel",), 32), name="rotary_tables",
    )(positions.reshape(b, s, 1), inv2, sign)


def _proj_spec(col_block):
    return pl.BlockSpec((1, SEQ_TILE, GROUP_WIDTH), lambda b, s: (b, s, col_block))


def _const_spec(shape):
    nd = len(shape)
    return pl.BlockSpec(shape, lambda b, s: (0,) * nd)


def _chunk_masks():
    t = jnp.arange(SEQ_TILE)
    same = (t[:, None] // CHUNK) == (t[None, :] // CHUNK)
    incl = (same & (t[None, :] <= t[:, None])).astype(F32)
    strict = (same & (t[None, :] < t[:, None])).astype(F32)
    return incl, strict


def _retention(proj, cosf, sinf, gain):
    b, s, _ = proj.shape
    t = SEQ_TILE
    f32 = F32
    log_gamma = jnp.log1p(-jnp.exp2(-5.0 - jnp.arange(N_HEADS, dtype=f32)))
    pos = jnp.arange(t, dtype=f32)
    cpos = jnp.arange(t) % CHUNK
    scale = HEAD_DIM ** -0.5
    dist = jnp.abs(pos[:, None] - pos[None, :])
    allowed = (pos[None, :] <= pos[:, None]) | ((jnp.arange(t)[:, None] // CHUNK) == (jnp.arange(t)[None, :] // CHUNK))
    mask = jnp.where(allowed[None], jnp.exp(log_gamma[:, None, None] * dist[None]), 0.0) * scale
    qdec = jnp.exp(log_gamma[:, None] * (pos + 1.0))[..., None] * scale
    kdec = jnp.exp(log_gamma[:, None] * (t - 1.0 - pos))[..., None]
    cdec = jnp.exp(log_gamma * t)[:, None, None]
    qdec = jnp.broadcast_to(qdec, (N_HEADS, t, HEAD_DIM))
    kdec = jnp.broadcast_to(kdec, (N_HEADS, t, HEAD_DIM))
    cdec = jnp.broadcast_to(cdec, (N_HEADS, 1, HEAD_DIM))
    del cpos

    def kern(q_ref, k_ref, v_ref, g_ref, cos_ref, sin_ref, m_ref, qd_ref, kd_ref, cd_ref, gain_ref, o_ref, st_ref):
        @pl.when(pl.program_id(1) == 0)
        def _():
            st_ref[...] = jnp.zeros_like(st_ref)

        cosv = cos_ref[0]
        sinv = sin_ref[0]
        for h in range(N_HEADS):
            sl = slice(h * HEAD_DIM, (h + 1) * HEAD_DIM)
            q = q_ref[0, :, sl].astype(F32)
            k = k_ref[0, :, sl].astype(F32)
            v = v_ref[0, :, sl]
            qr = q * cosv + pltpu.roll(q, HEAD_DIM // 2, 1) * sinv
            kr = k * cosv + pltpu.roll(k, HEAD_DIM // 2, 1) * sinv
            sc = _dot_nt(qr.astype(BF16), kr.astype(BF16)) * m_ref[h]
            st = st_ref[h]
            o = _dot(sc.astype(BF16), v) + _dot((qr * qd_ref[h]).astype(BF16), st.astype(BF16))
            st_ref[h] = cd_ref[h] * st + _dot_tn((kr * kd_ref[h]).astype(BF16), v)
            gate = _silu(g_ref[0, :, sl].astype(F32))
            o_ref[0, :, sl] = _head_norm_gate(o, gain_ref[:, sl], gate).astype(BF16)

    return pl.pallas_call(
        kern, grid=(b, s // t),
        in_specs=[_proj_spec(0), _proj_spec(1), _proj_spec(2), _proj_spec(3),
                  pl.BlockSpec((1, t, HEAD_DIM), lambda i, j: (i, j, 0)),
                  pl.BlockSpec((1, t, HEAD_DIM), lambda i, j: (i, j, 0)),
                  _const_spec((N_HEADS, t, t)), _const_spec((N_HEADS, t, HEAD_DIM)),
                  _const_spec((N_HEADS, t, HEAD_DIM)), _const_spec((N_HEADS, 1, HEAD_DIM)),
                  _const_spec((1, GROUP_WIDTH))],
        out_specs=pl.BlockSpec((1, t, GROUP_WIDTH), lambda i, j: (i, j, 0)),
        out_shape=jax.ShapeDtypeStruct((b, s, GROUP_WIDTH), BF16),
        scratch_shapes=[pltpu.VMEM((N_HEADS, HEAD_DIM, HEAD_DIM), F32)],
        compiler_params=_cparams(("parallel", "arbitrary"), 32), name="retention",
    )(proj, proj, proj, proj, cosf, sinf, mask, qdec, kdec, cdec, gain.reshape(1, GROUP_WIDTH))


def _gla(proj, small, w_up, b_up, gain):
    b, s, _ = proj.shape
    t = SEQ_TILE
    n_chunks = t // CHUNK
    incl, _ = _chunk_masks()
    w_up_pad = jnp.zeros((N_SMALL, N_HEADS * GLA_DK), F32).at[:GLA_LOWRANK].set(w_up)
    scale = GLA_DK ** -0.5

    def kern(qk_ref, v_ref, g_ref, sm_ref, wup_ref, bup_ref, tri_ref, gain_ref, o_ref, st_ref):
        @pl.when(pl.program_id(1) == 0)
        def _():
            st_ref[...] = jnp.zeros_like(st_ref)

        logit = _dot3(sm_ref[0], wup_ref[...]) + bup_ref[...]
        lg = _log_sigmoid(logit) * (1.0 / GLA_NORMALIZER)
        tri = tri_ref[...]
        bcum = _dot_exact_lhs(tri.astype(BF16), lg)
        e_pos = jnp.exp(bcum)
        e_neg = jnp.exp(-bcum)
        hr = range(N_HEADS)
        dsls = [slice(h * GLA_DK, (h + 1) * GLA_DK) for h in hr]
        vsls = [slice(h * HEAD_DIM, (h + 1) * HEAD_DIM) for h in hr]
        q_all = qk_ref[0, :, :N_HEADS * GLA_DK].astype(F32) * scale
        k_all = qk_ref[0, :, N_HEADS * GLA_DK:].astype(F32)
        qt_all = (q_all * e_pos).astype(BF16)
        kt_all = (k_all * e_neg).astype(BF16)
        scores = [(_dot_nt(qt_all[:, dsls[h]], kt_all[:, dsls[h]]) * tri).astype(BF16) for h in hr]
        o_intra = [_dot(scores[h], v_ref[0, :, vsls[h]]) for h in hr]
        sts = [st_ref[h] for h in hr]
        inter = [[] for _ in hr]
        for c in range(n_chunks):
            rows = slice(c * CHUNK, (c + 1) * CHUNK)
            b_last = bcum[(c + 1) * CHUNK - 1:(c + 1) * CHUNK, :]
            kd_all = (k_all[rows] * jnp.exp(b_last - bcum[rows])).astype(BF16)
            e_last = jnp.exp(b_last)
            for h in hr:
                inter[h].append(_dot_nt(qt_all[rows, dsls[h]], sts[h].astype(BF16)))
                sts[h] = sts[h] * e_last[:, dsls[h]] + _dot_tn(v_ref[0, rows, vsls[h]], kd_all[:, dsls[h]])
        for h in hr:
            st_ref[h] = sts[h]
            o = o_intra[h] + jnp.concatenate(inter[h], axis=0)
            gate = _silu(g_ref[0, :, vsls[h]].astype(F32))
            o_ref[0, :, vsls[h]] = _head_norm_gate(o, gain_ref[:, vsls[h]], gate).astype(BF16)

    return pl.pallas_call(
        kern, grid=(b, s // t),
        in_specs=[_proj_spec(4), _proj_spec(5), _proj_spec(6),
                  pl.BlockSpec((1, t, N_SMALL), lambda i, j: (i, j, 0)),
                  _const_spec((N_SMALL, N_HEADS * GLA_DK)), _const_spec((1, N_HEADS * GLA_DK)),
                  _const_spec((t, t)), _const_spec((1, GROUP_WIDTH))],
        out_specs=pl.BlockSpec((1, t, GROUP_WIDTH), lambda i, j: (i, j, 0)),
        out_shape=jax.ShapeDtypeStruct((b, s, GROUP_WIDTH), BF16),
        scratch_shapes=[pltpu.VMEM((N_HEADS, HEAD_DIM, GLA_DK), F32)],
        compiler_params=_cparams(("parallel", "arbitrary"), 32), name="gla",
    )(proj, proj, proj, small, w_up_pad, b_up.reshape(1, -1), incl, gain.reshape(1, GROUP_WIDTH))


def _stick_breaking(proj, gain):
    b, s, _ = proj.shape
    t = SEQ_TILE
    idx = jnp.arange(t)
    later = (idx[:, None] > idx[None, :]).astype(BF16)
    causal = (idx[None, :] < idx[:, None]).astype(F32)

    def kern(q_ref, k_ref, v_ref, later_ref, causal_ref, gain_ref, o_ref, acc_ref, carry_ref):
        qi = pl.program_id(1)

        def key_block(row, diag):
            heads = [slice(h * HEAD_DIM, (h + 1) * HEAD_DIM) for h in range(N_HEADS)]
            zs = [_dot_nt(q_ref[0, :, sl], k_ref[0, pl.ds(row, t), sl]) for sl in heads]
            log_betas, sums = [], []
            for h in range(N_HEADS):
                z = zs[h]
                sp = jnp.log(1.0 + jnp.exp(-jnp.abs(z)))
                log_beta = jnp.minimum(z, 0.0) - sp
                log_keep = log_beta - z
                if diag:
                    log_keep = log_keep * causal_ref[...]
                log_later = _dot(log_keep.astype(BF16), later_ref[...])
                if not diag:
                    log_later = log_later + carry_ref[h]
                log_betas.append(log_beta + log_later)
                row_sum = jnp.sum(log_keep, axis=-1, keepdims=True)
                if diag:
                    carry_ref[h] = row_sum
                else:
                    carry_ref[h] += row_sum
            for h in range(N_HEADS):
                w = jnp.exp(log_betas[h])
                if diag:
                    w = w * causal_ref[...]
                pv = _dot(w.astype(BF16), v_ref[0, pl.ds(row, t), heads[h]])
                if diag:
                    acc_ref[h] = pv
                else:
                    acc_ref[h] += pv

        key_block(pl.multiple_of(qi * t, t), True)

        def body(i, c):
            key_block(pl.multiple_of((qi - 1 - i) * t, t), False)
            return c

        lax.fori_loop(0, qi, body, 0)
        for h in range(N_HEADS):
            sl = slice(h * HEAD_DIM, (h + 1) * HEAD_DIM)
            o_ref[0, :, sl] = _head_norm_gate(acc_ref[h], gain_ref[:, sl], 1.0).astype(BF16)

    return pl.pallas_call(
        kern, grid=(b, s // t),
        in_specs=[_proj_spec(7),
                  pl.BlockSpec((1, s, GROUP_WIDTH), lambda i, j: (i, 0, 8)),
                  pl.BlockSpec((1, s, GROUP_WIDTH), lambda i, j: (i, 0, 9)),
                  _const_spec((t, t)), _const_spec((t, t)), _const_spec((1, GROUP_WIDTH))],
        out_specs=pl.BlockSpec((1, t, GROUP_WIDTH), lambda i, j: (i, j, 0)),
        out_shape=jax.ShapeDtypeStruct((b, s, GROUP_WIDTH), BF16),
        scratch_shapes=[pltpu.VMEM((N_HEADS, t, HEAD_DIM), F32), pltpu.VMEM((N_HEADS, t, 1), F32)],
        compiler_params=_cparams(("parallel", "arbitrary"), 40), name="stick_breaking",
    )(proj, proj, proj, later, causal, gain.reshape(1, GROUP_WIDTH))


def _deltanet(proj, small, conv_w, a_log, dt_bias, gain):
    b, s, _ = proj.shape
    t = SEQ_TILE
    n_chunks = t // CHUNK
    incl, strict = _chunk_masks()
    a_pad = jnp.zeros((1, N_SMALL), F32).at[0, LANE_DECAY:LANE_DECAY + N_HEADS].set(a_log)
    dt_pad = jnp.zeros((1, N_SMALL), F32).at[0, LANE_DECAY:LANE_DECAY + N_HEADS].set(dt_bias)
    sel = jnp.zeros((8, N_SMALL), F32).at[jnp.arange(N_HEADS), LANE_DECAY + jnp.arange(N_HEADS)].set(1.0).astype(BF16)
    qscale = HEAD_DIM ** -0.5
    pad = 8

    def kern(q_ref, k_ref, v_ref, z_ref, sm_ref, cw_ref, alog_ref, dtb_ref, incl_ref, strict_ref, sel_ref,
             gain_ref, o_ref, st_ref, tail_ref):
        @pl.when(pl.program_id(1) == 0)
        def _():
            st_ref[...] = jnp.zeros_like(st_ref)
            tail_ref[...] = jnp.zeros_like(tail_ref)

        convd = []
        row = lax.broadcasted_iota(jnp.int32, (pad, GROUP_WIDTH), 0)
        for i, ref in enumerate((q_ref, k_ref, v_ref)):
            x = ref[0].astype(F32)
            csl = slice(i * GROUP_WIDTH, (i + 1) * GROUP_WIDTH)
            y = x * cw_ref[DN_CONV - 1:DN_CONV, csl]
            x8 = x[:pad]
            y8 = y[:pad]
            tail = tail_ref[i]
            for back in range(1, DN_CONV):
                w_tap = cw_ref[DN_CONV - 1 - back:DN_CONV - back, csl]
                y = y + pltpu.roll(x, back, 0) * w_tap
                y8 = y8 + jnp.where(row < back, pltpu.roll(tail, back, 0), pltpu.roll(x8, back, 0)) * w_tap
            y = jnp.concatenate([y8, y[pad:]], axis=0)
            tail_ref[i] = x[t - pad:]
            convd.append(_silu(y))
        cq, ck, cv = convd

        sm = sm_ref[0]
        beta_all = _sigmoid(sm)
        g_all = -jnp.exp(alog_ref[...]) * _softplus(sm + dtb_ref[...])
        incl_m = incl_ref[...]
        strict_m = strict_ref[...]
        gcum = _dot_exact_lhs(incl_m.astype(BF16), g_all)
        ghi, gmid, glo = _split3(gcum)
        sel_m = sel_ref[...]
        grow = _dot_nt(sel_m, ghi) + _dot_nt(sel_m, gmid) + _dot_nt(sel_m, glo)
        egc = jnp.exp(gcum)

        hr = range(N_HEADS)
        sls = [slice(h * HEAD_DIM, (h + 1) * HEAD_DIM) for h in hr]
        gcs = [gcum[:, LANE_DECAY + h:LANE_DECAY + h + 1] for h in hr]
        egs = [egc[:, LANE_DECAY + h:LANE_DECAY + h + 1] for h in hr]
        betas = [beta_all[:, LANE_BETA + h:LANE_BETA + h + 1] for h in hr]
        qs, ks, k16s, kbs, decays, n1s, scores = [], [], [], [], [], [], []
        for h in hr:
            q = cq[:, sls[h]]
            k = ck[:, sls[h]]
            q = q * lax.rsqrt(jnp.sum(q * q, axis=-1, keepdims=True) + 1e-6) * qscale
            k = k * lax.rsqrt(jnp.sum(k * k, axis=-1, keepdims=True) + 1e-6)
            diff = gcs[h] - grow[h:h + 1, :]
            decay = jnp.exp(jnp.minimum(diff, 0.0)) * incl_m
            kb = k * betas[h]
            k16 = k.astype(BF16)
            n1s.append(-(_dot_nt(kb.astype(BF16), k16) * decay * strict_m))
            scores.append((_dot_nt(q.astype(BF16), k16) * decay).astype(BF16))
            qs.append(q); ks.append(k); k16s.append(k16); kbs.append(kb)
        n16s = [n.astype(BF16) for n in n1s]
        a_s = [_dot(n, n) for n in n16s]
        p_s = list(n1s)
        levels = int(math.log2(CHUNK)) - 1
        for lvl in range(levels):
            for h in hr:
                a16 = a_s[h].astype(BF16)
                if lvl < levels - 1:
                    prod = _dot(jnp.concatenate([p_s[h].astype(BF16), a16], axis=0), a16)
                    p_s[h] = p_s[h] + a_s[h] + prod[:t]
                    a_s[h] = prod[t:]
                else:
                    p_s[h] = p_s[h] + a_s[h] + _dot(p_s[h].astype(BF16), a16)
        us, w16s, qes = [], [], []
        for h in hr:
            rhs = jnp.concatenate([cv[:, sls[h]] * betas[h], kbs[h] * egs[h]], axis=-1)
            sol = rhs + _dot(p_s[h].astype(BF16), rhs.astype(BF16))
            us.append(sol[:, :HEAD_DIM])
            w16s.append(sol[:, HEAD_DIM:].astype(BF16))
            qes.append((qs[h] * egs[h]).astype(BF16))
        chunk_rows = [slice(c * CHUNK, (c + 1) * CHUNK) for c in range(n_chunks)]
        steps = [[] for _ in hr]
        for c in range(n_chunks):
            rows = chunk_rows[c]
            last = slice((c + 1) * CHUNK - 1, (c + 1) * CHUNK)
            for h in hr:
                g_last = gcs[h][last, :]
                kd = (ks[h][rows] * jnp.exp(g_last - gcs[h][rows])).astype(BF16)
                wu = jnp.concatenate([w16s[h][rows], us[h][rows].astype(BF16)], axis=-1)
                prod = _dot_tn(kd, wu)
                steps[h].append((jnp.exp(g_last), prod[:, :HEAD_DIM].astype(BF16), prod[:, HEAD_DIM:]))
        states = [[st_ref[h]] for h in hr]
        for c in range(n_chunks):
            for h in hr:
                g, kw16, ku = steps[h][c]
                s_in = states[h][c]
                states[h].append(g * s_in - _dot(kw16, s_in.astype(BF16)) + ku)
        for h in hr:
            st_ref[h] = states[h][n_chunks]
            v_new, inter = [], []
            for c in range(n_chunks):
                rows = chunk_rows[c]
                both = _dot(jnp.concatenate([w16s[h][rows], qes[h][rows]], axis=0), states[h][c].astype(BF16))
                v_new.append(us[h][rows] - both[:CHUNK])
                inter.append(both[CHUNK:])
            vn_all = jnp.concatenate(v_new, axis=0).astype(BF16)
            o = jnp.concatenate(inter, axis=0) + _dot(scores[h], vn_all)
            gate = _silu(z_ref[0, :, sls[h]].astype(F32))
            o_ref[0, :, sls[h]] = _head_norm_gate(o, gain_ref[:, sls[h]], gate).astype(BF16)

    return pl.pallas_call(
        kern, grid=(b, s // t),
        in_specs=[_proj_spec(10), _proj_spec(11), _proj_spec(12), _proj_spec(13),
                  pl.BlockSpec((1, t, N_SMALL), lambda i, j: (i, j, 0)),
                  _const_spec((DN_CONV, 3 * GROUP_WIDTH)), _const_spec((1, N_SMALL)), _const_spec((1, N_SMALL)),
                  _const_spec((t, t)), _const_spec((t, t)), _const_spec((8, N_SMALL)), _const_spec((1, GROUP_WIDTH))],
        out_specs=pl.BlockSpec((1, t, GROUP_WIDTH), lambda i, j: (i, j, 0)),
        out_shape=jax.ShapeDtypeStruct((b, s, GROUP_WIDTH), BF16),
        scratch_shapes=[pltpu.VMEM((N_HEADS, HEAD_DIM, HEAD_DIM), F32), pltpu.VMEM((3, pad, GROUP_WIDTH), F32)],
        compiler_params=_cparams(("parallel", "arbitrary"), 40), name="deltanet",
    )(proj, proj, proj, proj, small, conv_w, a_pad, dt_pad, incl, strict, sel, gain.reshape(1, GROUP_WIDTH))


def _cross_block(x, kv, gain, w_q, w_o, layer, b, s, *, tq, vmem_mib):
    m, d = x.shape
    tq = min(tq, s)
    tiles = s // tq
    rows = min(256, tq)
    scale = CROSS_HEAD_DIM ** -0.5

    def kern(x_ref, g_ref, wq_ref, wo_ref, kv_ref, o_ref, h_ref, q_ref, att_ref):
        _rms_rows(x_ref, h_ref, g_ref[...], rows, tq, BF16)
        q_ref[...] = (_dot(h_ref[...], wq_ref[...]) * scale).astype(BF16)
        for h in range(N_CROSS_HEADS):
            sl = slice(h * CROSS_HEAD_DIM, (h + 1) * CROSS_HEAD_DIM)
            vsl = slice(d + h * CROSS_HEAD_DIM, d + (h + 1) * CROSS_HEAD_DIM)
            sc = _dot_nt(q_ref[:, sl], kv_ref[0, :, sl])
            p = jnp.exp(sc - jnp.max(sc, axis=-1, keepdims=True))
            denom = jnp.sum(p, axis=-1, keepdims=True)
            att_ref[:, sl] = (_dot(p.astype(BF16), kv_ref[0, :, vsl]) / denom).astype(BF16)
        o_ref[...] = x_ref[...] + _dot(att_ref[...], wo_ref[...])

    resident = dict(pipeline_mode=pl.Buffered(1))
    return pl.pallas_call(
        kern, grid=(b, tiles),
        in_specs=[pl.BlockSpec((tq, d), lambda i, j: (i * tiles + j, 0)),
                  pl.BlockSpec((1, d), lambda i, j: (0, 0)),
                  pl.BlockSpec((None, d, d), lambda i, j: (layer, 0, 0), **resident),
                  pl.BlockSpec((None, d, d), lambda i, j: (layer, 0, 0), **resident),
                  pl.BlockSpec((1, N_MEM, 2 * d), lambda i, j: (i, 0, 0))],
        out_specs=pl.BlockSpec((tq, d), lambda i, j: (i * tiles + j, 0)),
        out_shape=jax.ShapeDtypeStruct((m, d), F32),
        scratch_shapes=[pltpu.VMEM((tq, d), BF16), pltpu.VMEM((tq, d), BF16), pltpu.VMEM((tq, d), BF16)],
        compiler_params=_cparams(("parallel", "arbitrary"), vmem_mib), name="cross_block",
    )(x, gain.reshape(1, d), w_q, w_o, kv)


def _reorder_w_in(w_in):
    lr0 = 4 * GROUP_WIDTH + 2 * N_HEADS * GLA_DK + 2 * GROUP_WIDTH
    sb0 = lr0 + GLA_LOWRANK
    db0 = sb0 + 3 * GROUP_WIDTH + 3 * GROUP_WIDTH + GROUP_WIDTH
    main = jnp.concatenate([w_in[..., :lr0].astype(BF16), w_in[..., sb0:db0].astype(BF16)], axis=-1)
    small = jnp.concatenate(
        [w_in[..., lr0:sb0], w_in[..., db0:db0 + 2 * N_HEADS],
         jnp.zeros(w_in.shape[:-1] + (N_SMALL - GLA_LOWRANK - 2 * N_HEADS,), w_in.dtype)], axis=-1).astype(BF16)
    return main, small


def _layer(x2, mem2, b, s, cosf, sinf, p, w, layer, final_gain):
    m, d = x2.shape
    col_scale = jnp.ones((N_MAIN,), F32).at[7 * GROUP_WIDTH:8 * GROUP_WIDTH].set(HEAD_DIM ** -0.5)
    proj, small = _matmul("in_proj", [x2], w["in_main"], layer, gain=p["ln_mix"], w_small=w["in_small"],
                          col_scale=col_scale, out_dtype=BF16, tm=1024, tn=1792, vmem_mib=56)
    proj = proj.reshape(b, s, N_MAIN)
    small = small.reshape(b, s, N_SMALL)
    gains = jnp.split(p["mix_head_norm"], 4)
    y_ret = _retention(proj, cosf, sinf, gains[0])
    y_gla = _gla(proj, small, p["gla_w_up"], p["gla_b_up"], gains[1])
    y_sb = _stick_breaking(proj, gains[2])
    y_dn = _deltanet(proj, small, p["dn_conv_w"], p["dn_a_log"], p["dn_dt_bias"], gains[3])
    parts = [y.reshape(m, GROUP_WIDTH) for y in (y_ret, y_gla, y_sb, y_dn)]
    x2 = _matmul("mix_out", parts, w["mix_out"], layer, res=x2, out_dtype=F32, tm=512, tn=2048, vmem_mib=48)

    kv = _matmul("cross_kv", [mem2], w["cross_kv"], layer, gain=p["ln_mem"], out_dtype=BF16,
                 tm=1024, tn=1024, vmem_mib=48)
    x2 = _cross_block(x2, kv.reshape(b, N_MEM, 2 * d), p["ln_cross"], w["cross_q"], w["cross_o"], layer, b, s,
                      tq=512, vmem_mib=56)
    return _mlp(x2, p["ln_mlp"], w["mlp_up"], w["mlp_down"], layer, final_gain, tm=1024, tf=512, vmem_mib=56)


def kernel(x, mem, positions, ln_mix, w_in, gla_w_up, gla_b_up, dn_conv_w, dn_a_log, dn_dt_bias, mix_head_norm,
           w_mix_out, ln_cross, ln_mem, w_cross_q, w_cross_kv, w_cross_o, ln_mlp, w_mlp_up, w_mlp_down, ln_final):
    cosf, sinf = _rotary_tables(positions)
    small_params = dict(ln_mix=ln_mix, gla_w_up=gla_w_up, gla_b_up=gla_b_up, dn_conv_w=dn_conv_w,
                        dn_a_log=dn_a_log, dn_dt_bias=dn_dt_bias, mix_head_norm=mix_head_norm,
                        ln_cross=ln_cross, ln_mem=ln_mem, ln_mlp=ln_mlp)
    in_main, in_small = _reorder_w_in(w_in)
    weights = dict(in_main=in_main, in_small=in_small, mix_out=w_mix_out.astype(BF16),
                   cross_q=w_cross_q.astype(BF16), cross_kv=w_cross_kv.astype(BF16),
                   cross_o=w_cross_o.astype(BF16), mlp_up=w_mlp_up.astype(BF16), mlp_down=w_mlp_down.astype(BF16))
    b, s, d = x.shape
    depth = ln_mix.shape[0]
    x2 = x.reshape(b * s, d)
    mem2 = mem.reshape(-1, d)
    for layer in range(depth):
        x2 = _layer(x2, mem2, b, s, cosf, sinf, {k: v[layer] for k, v in small_params.items()}, weights, layer,
                    ln_final if layer == depth - 1 else None)
    return x2.reshape(b, s, d)
```
